```python
import math
import jax, jax.numpy as jnp
from jax import lax
import numpy as np

D_MODEL = 1024
BATCH = 2
SEQ = 8192
DEPTH = 4

N_META = 16
BLOCK = 128
N_PAD = BLOCK - N_META
SB_HEADS = 8
SB_HEAD_DIM = 64
SB_WIDTH = SB_HEADS * SB_HEAD_DIM
SSM_HEADS = 16
SSM_HEAD_DIM = 64
SSM_WIDTH = SSM_HEADS * SSM_HEAD_DIM
SSM_GROUPS = 2
SSM_STATE = 128
CONV_K = 4
CONV_CH = SSM_WIDTH + 2 * SSM_GROUPS * SSM_STATE
MLA_HEADS = 8
Q_LORA = 384
KV_LORA = 256
QK_NOPE = 64
QK_ROPE = 32
V_HEAD = 64
MLA_WIDTH = MLA_HEADS * V_HEAD
ROPE_BASE = 10000.0
MIX_WIDTH = SB_WIDTH + SSM_WIDTH + MLA_WIDTH
D_FF = -(-8 * D_MODEL // (3 * 256)) * 256
IN_SPLITS = (SB_WIDTH, SB_WIDTH, SB_WIDTH, SSM_WIDTH, CONV_CH, SSM_HEADS, Q_LORA, KV_LORA, QK_ROPE)
IN_DIM = 3 * SB_WIDTH + SSM_WIDTH + CONV_CH + SSM_HEADS + Q_LORA + KV_LORA + QK_ROPE
EPS = 1e-6

kernel_name = "hymba_sb_ssd_mla_hybrid"

F32 = jnp.float32


def rmsnorm(x, g):
    xf = x.astype(F32)
    y = xf * lax.rsqrt(jnp.mean(xf * xf, axis=-1, keepdims=True) + EPS)
    return (y * g.astype(F32)).astype(x.dtype)


def rope_tables(length):
    pos = (jnp.arange(length) - N_PAD).astype(F32)
    inv = ROPE_BASE ** (-jnp.arange(0, QK_ROPE, 2, dtype=F32) / QK_ROPE)
    ang = pos[:, None] * inv[None, :]
    return jnp.cos(ang), jnp.sin(ang)


def apply_rope(x, cos, sin):
    xf = x.astype(F32)
    x1, x2 = jnp.split(xf, 2, axis=-1)
    return jnp.concatenate([x1 * cos - x2 * sin, x1 * sin + x2 * cos], axis=-1).astype(x.dtype)


def sweep_query_blocks(block_fn, q):
    b, L, h, d = q.shape
    nb = L // BLOCK
    qb = jnp.moveaxis(q.reshape(b, nb, BLOCK, h, d), 1, 0)
    starts = jnp.arange(nb) * BLOCK
    out = lax.map(lambda a: block_fn(a[0], a[1]), (qb, starts))
    return jnp.moveaxis(out, 0, 1).reshape(b, L, h, out.shape[-1])


def stick_breaking_attention(q, k, v, valid):
    L = k.shape[1]
    scale = 1.0 / math.sqrt(SB_HEAD_DIM)
    kidx = jnp.arange(L)
    kf = k.astype(F32)
    vf = v.astype(F32)

    def block(qb, start):
        qidx = start + jnp.arange(BLOCK)
        mask = (kidx[None, :] < qidx[:, None]) & valid[None, :]
        z = jnp.einsum('bqhd,bkhd->bhqk', qb.astype(F32), kf) * scale
        log_1m = jnp.where(mask, jax.nn.log_sigmoid(-z), 0.0)
        between = lax.cumsum(log_1m, axis=3, reverse=True) - log_1m
        w = jnp.exp(jnp.where(mask, jax.nn.log_sigmoid(z) + between, -jnp.inf))
        return jnp.einsum('bhqk,bkhd->bqhd', w, vf)

    return sweep_query_blocks(block, q).astype(q.dtype)


def mla_attention(q, k, v, valid):
    L = k.shape[1]
    scale = 1.0 / math.sqrt(QK_NOPE + QK_ROPE)
    kidx = jnp.arange(L)
    kf = k.astype(F32)
    vf = v.astype(F32)

    def block(qb, start):
        qidx = start + jnp.arange(BLOCK)
        mask = (kidx[None, :] <= qidx[:, None]) & valid[None, :]
        s = jnp.einsum('bqhd,bkhd->bhqk', qb.astype(F32), kf) * scale
        p = jax.nn.softmax(jnp.where(mask, s, -1e30), axis=-1)
        return jnp.einsum('bhqk,bkhd->bqhd', p, vf)

    return sweep_query_blocks(block, q).astype(q.dtype)


def ssd_chunked(xh, dt, a, bm, cm):
    b, L, H, P = xh.shape
    G, N = bm.shape[2], bm.shape[3]
    J = H // G
    nc = L // BLOCK
    X = (xh * dt[..., None]).reshape(b, nc, BLOCK, G, J, P)
    dA = (dt * a).reshape(b, nc, BLOCK, G, J)
    Acs = jnp.cumsum(dA, axis=2)
    Bc = bm.reshape(b, nc, BLOCK, G, N)
    Cc = cm.reshape(b, nc, BLOCK, G, N)
    tri = jnp.tril(jnp.ones((BLOCK, BLOCK), dtype=bool))
    seg = Acs[:, :, :, None] - Acs[:, :, None, :]
    Ldec = jnp.exp(jnp.where(tri[None, None, :, :, None, None], seg, -jnp.inf))
    CB = jnp.einsum('bclgn,bcsgn->bclsg', Cc, Bc)
    y_diag = jnp.einsum('bclsgj,bcsgjp->bclgjp', CB[..., None] * Ldec, X)
    decay = jnp.exp(Acs[:, :, -1:] - Acs)
    states = jnp.einsum('bclgn,bclgjp->bcgjpn', Bc, X * decay[..., None])
    chunk_decay = jnp.exp(Acs[:, :, -1])

    def step(carry, inp):
        cd, st = inp
        return carry * cd[..., None, None] + st, carry

    init = jnp.zeros((b, G, J, P, N), F32)
    _, prev = lax.scan(step, init, (jnp.moveaxis(chunk_decay, 1, 0), jnp.moveaxis(states, 1, 0)))
    prev = jnp.moveaxis(prev, 0, 1)
    y_off = jnp.einsum('bclgn,bcgjpn->bclgjp', Cc, prev) * jnp.exp(Acs)[..., None]
    return (y_diag + y_off).reshape(b, L, H, P)


def ssd_mixer(z, xbc, dt_raw, conv_w, conv_b, dt_bias, a_log, d_skip, norm_g, valid):
    b, L, _ = xbc.shape
    xbc = xbc * valid[None, :, None].astype(xbc.dtype)
    xbc = lax.conv_general_dilated(
        xbc, conv_w[:, None, :].astype(xbc.dtype), window_strides=(1,), padding=[(CONV_K - 1, 0)],
        dimension_numbers=('NWC', 'WIO', 'NWC'), feature_group_count=CONV_CH) + conv_b.astype(xbc.dtype)
    xbc = jax.nn.silu(xbc)
    xs, bm, cm = jnp.split(xbc, [SSM_WIDTH, SSM_WIDTH + SSM_GROUPS * SSM_STATE], axis=-1)
    xh = xs.reshape(b, L, SSM_HEADS, SSM_HEAD_DIM).astype(F32)
    bm = bm.reshape(b, L, SSM_GROUPS, SSM_STATE).astype(F32)
    cm = cm.reshape(b, L, SSM_GROUPS, SSM_STATE).astype(F32)
    dt = jax.nn.softplus(dt_raw.astype(F32) + dt_bias.astype(F32)) * valid[None, :, None].astype(F32)
    a = -jnp.exp(a_log.astype(F32))
    y = ssd_chunked(xh, dt, a, bm, cm) + xh * d_skip.astype(F32)[:, None]
    y = y.reshape(b, L, SSM_WIDTH)
    return rmsnorm(y * jax.nn.silu(z.astype(F32)), norm_g).astype(z.dtype)


def setup_inputs(seed: int = 0) -> dict:
    key = jax.random.key(seed)
    ks = jax.random.split(key, 24)

    def dense(k, shape, fan_in):
        return jax.random.normal(k, shape, F32) * fan_in ** -0.5

    def gain(k, shape):
        return 1.0 + 0.02 * jax.random.normal(k, shape, F32)

    dt0 = jnp.exp(jax.random.uniform(ks[6], (DEPTH, SSM_HEADS), F32, math.log(1e-3), math.log(1e-1)))
    return {
        "x": jax.random.normal(ks[0], (BATCH, SEQ, D_MODEL), F32),
        "meta_tokens": jax.random.normal(ks[1], (N_META, D_MODEL), F32),
        "mix_norm_g": gain(ks[2], (DEPTH, D_MODEL)),
        "w_in": dense(ks[3], (DEPTH, D_MODEL, IN_DIM), D_MODEL),
        "conv_w": dense(ks[4], (DEPTH, CONV_K, CONV_CH), CONV_K),
        "conv_b": 0.02 * jax.random.normal(ks[5], (DEPTH, CONV_CH), F32),
        "dt_bias": dt0 + jnp.log(-jnp.expm1(-dt0)),
        "a_log": jnp.log(jax.random.uniform(ks[7], (DEPTH, SSM_HEADS), F32, 1.0, 16.0)),
        "d_skip": gain(ks[8], (DEPTH, SSM_HEADS)),
        "ssm_norm_g": gain(ks[9], (DEPTH, SSM_WIDTH)),
        "q_norm_g": gain(ks[10], (DEPTH, Q_LORA)),
        "w_uq": dense(ks[11], (DEPTH, Q_LORA, MLA_HEADS * (QK_NOPE + QK_ROPE)), Q_LORA),
        "kv_norm_g": gain(ks[12], (DEPTH, KV_LORA)),
        "w_ukv": dense(ks[13], (DEPTH, KV_LORA, MLA_HEADS * (QK_NOPE + V_HEAD)), KV_LORA),
        "w_out": dense(ks[14], (DEPTH, MIX_WIDTH, D_MODEL), MIX_WIDTH),
        "ffn_norm_g": gain(ks[15], (DEPTH, D_MODEL)),
        "w_gate": dense(ks[16], (DEPTH, D_MODEL, D_FF), D_MODEL),
        "w_up": dense(ks[17], (DEPTH, D_MODEL, D_FF), D_MODEL),
        "w_down": dense(ks[18], (DEPTH, D_FF, D_MODEL), D_FF),
        "final_norm_g": gain(ks[19], (D_MODEL,)),
    }


def reference(x, meta_tokens, mix_norm_g, w_in, conv_w, conv_b, dt_bias, a_log, d_skip, ssm_norm_g,
              q_norm_g, w_uq, kv_norm_g, w_ukv, w_out, ffn_norm_g, w_gate, w_up, w_down, final_norm_g):
    b = x.shape[0]
    dt = x.dtype
    h = jnp.concatenate([
        jnp.zeros((b, N_PAD, D_MODEL), dt),
        jnp.broadcast_to(meta_tokens.astype(dt)[None], (b, N_META, D_MODEL)),
        x], axis=1)
    L = h.shape[1]
    valid = jnp.arange(L) >= N_PAD
    cos, sin = rope_tables(L)
    split_at = [int(o) for o in np.cumsum(IN_SPLITS)[:-1]]

    for i in range(DEPTH):
        hn = rmsnorm(h, mix_norm_g[i])
        proj = hn @ w_in[i].astype(dt)
        sb_q, sb_k, sb_v, ssm_z, ssm_xbc, ssm_dt, c_q, c_kv, k_r = jnp.split(proj, split_at, axis=-1)

        o_sb = stick_breaking_attention(
            sb_q.reshape(b, L, SB_HEADS, SB_HEAD_DIM),
            sb_k.reshape(b, L, SB_HEADS, SB_HEAD_DIM),
            sb_v.reshape(b, L, SB_HEADS, SB_HEAD_DIM), valid).reshape(b, L, SB_WIDTH)

        o_ssm = ssd_mixer(ssm_z, ssm_xbc, ssm_dt, conv_w[i], conv_b[i], dt_bias[i], a_log[i],
                          d_skip[i], ssm_norm_g[i], valid)

        q = (rmsnorm(c_q, q_norm_g[i]) @ w_uq[i].astype(dt)).reshape(b, L, MLA_HEADS, QK_NOPE + QK_ROPE)
        q_nope, q_rope = jnp.split(q, [QK_NOPE], axis=-1)
        q_rope = apply_rope(q_rope, cos[None, :, None, :], sin[None, :, None, :])
        kv = (rmsnorm(c_kv, kv_norm_g[i]) @ w_ukv[i].astype(dt)).reshape(b, L, MLA_HEADS, QK_NOPE + V_HEAD)
        k_nope, v_mla = jnp.split(kv, [QK_NOPE], axis=-1)
        k_rope = apply_rope(k_r, cos[None], sin[None])
        q_mla = jnp.concatenate([q_nope, q_rope], axis=-1)
        k_mla = jnp.concatenate(
            [k_nope, jnp.broadcast_to(k_rope[:, :, None, :], (b, L, MLA_HEADS, QK_ROPE))], axis=-1)
        o_mla = mla_attention(q_mla, k_mla, v_mla, valid).reshape(b, L, MLA_WIDTH)

        h = h + jnp.concatenate([o_sb, o_ssm, o_mla], axis=-1) @ w_out[i].astype(dt)

        fn = rmsnorm(h, ffn_norm_g[i])
        h = h + (jax.nn.silu(fn @ w_gate[i].astype(dt)) * (fn @ w_up[i].astype(dt))) @ w_down[i].astype(dt)

    return rmsnorm(h, final_norm_g)[:, BLOCK:, :]
```

```python
import functools
import math

import jax
import jax.numpy as jnp
from jax import lax
from jax.experimental import pallas as pl
from jax.experimental.pallas import tpu as pltpu

F32 = jnp.float32
BF16 = jnp.bfloat16

D_MODEL = 1024
N_META = 16
BLOCK = 128
N_PAD = BLOCK - N_META
SB_HEADS = 8
SB_HEAD_DIM = 64
SB_WIDTH = SB_HEADS * SB_HEAD_DIM
SSM_HEADS = 16
SSM_HEAD_DIM = 64
SSM_WIDTH = SSM_HEADS * SSM_HEAD_DIM
SSM_GROUPS = 2
SSM_STATE = 128
CONV_K = 4
CONV_CH = SSM_WIDTH + 2 * SSM_GROUPS * SSM_STATE
MLA_HEADS = 8
Q_LORA = 384
KV_LORA = 256
QK_NOPE = 64
QK_ROPE = 32
V_HEAD = 64
MLA_WIDTH = MLA_HEADS * V_HEAD
ROPE_BASE = 10000.0
MIX_WIDTH = SB_WIDTH + SSM_WIDTH + MLA_WIDTH
D_FF = 2816
EPS = 1e-6

LANES = 128
ROPE_LANE0 = QK_NOPE
DT_LANE0 = QK_NOPE + QK_ROPE
HIST = 8
SB_EXIT = -110.0
NEG_BIG = -1e30
VMEM_LIMIT = 56 * 1024 * 1024

IN_SEGS = (("q", SB_WIDTH, BF16), ("k", SB_WIDTH, BF16), ("v", SB_WIDTH, BF16),
           ("z", SSM_WIDTH, F32), ("xbc", CONV_CH, F32), ("cq", Q_LORA, F32),
           ("ckv", KV_LORA, F32), ("ma", LANES, F32), ("mb", LANES, F32))
IN_WIDTH = sum(s[1] for s in IN_SEGS)


def _cparams(*sem):
    return pltpu.CompilerParams(dimension_semantics=sem, vmem_limit_bytes=VMEM_LIMIT)


def _const_spec(shape):
    nd = len(shape)
    return pl.BlockSpec(shape, lambda *_: (0,) * nd, pipeline_mode=pl.Buffered(1))


def _rms(x, g):
    return x * lax.rsqrt(jnp.mean(x * x, axis=-1, keepdims=True) + EPS) * g


def _silu(x):
    return x * (1.0 / (1.0 + jnp.exp(-x)))


def _softplus(x):
    return jnp.maximum(x, 0.0) + jnp.log(1.0 + jnp.exp(-jnp.abs(x)))


def _split3(x):
    hi = x.astype(BF16)
    r1 = x - hi.astype(F32)
    mid = r1.astype(BF16)
    lo = (r1 - mid.astype(F32)).astype(BF16)
    return hi, mid, lo


def _dot(a, b):
    return jnp.dot(a, b, preferred_element_type=F32)


def _dot_nt(a, b):
    return lax.dot_general(a, b, (((1,), (1,)), ((), ())), preferred_element_type=F32)


def _dot_f32_lhs(x, rhs01):
    return sum(_dot(p, rhs01) for p in _split3(x))


def _dot_f32_rhs(lhs01, x):
    return sum(_dot(lhs01, p) for p in _split3(x))


def _inproj_body(h_ref, g_ref, w_ref, *out_refs):
    hn = _rms(h_ref[...], g_ref[...]).astype(BF16)
    off = 0
    for ref in out_refs:
        width = ref.shape[-1]
        ref[...] = _dot(hn, w_ref[:, off:off + width]).astype(ref.dtype)
        off += width


def _inproj(h, g, w, tm):
    rows = h.shape[0]
    return pl.pallas_call(
        _inproj_body,
        grid=(rows // tm,),
        in_specs=[pl.BlockSpec((tm, D_MODEL), lambda i: (i, 0)),
                  _const_spec((1, D_MODEL)),
                  _const_spec((D_MODEL, IN_WIDTH))],
        out_specs=[pl.BlockSpec((tm, wd), lambda i: (i, 0)) for _, wd, _ in IN_SEGS],
        out_shape=[jax.ShapeDtypeStruct((rows, wd), dt) for _, wd, dt in IN_SEGS],
        compiler_params=_cparams("parallel"),
        name="inproj",
    )(h, g, w)


def _sb_body(q_ref, k_ref, v_ref, o_ref, *, tq):
    tk = LANES
    i = pl.program_id(2)
    lane = lax.broadcasted_iota(jnp.int32, (tq, LANES), 1)
    q = q_ref[...]
    qz = jnp.zeros_like(q)
    q_heads = (jnp.where(lane < SB_HEAD_DIM, q, qz), jnp.where(lane >= SB_HEAD_DIM, q, qz))
    ur = lax.broadcasted_iota(jnp.int32, (tk, 2 * tk), 0)
    uc = lax.broadcasted_iota(jnp.int32, (tk, 2 * tk), 1)
    u = jnp.where(ur >= uc, 1.0, jnp.where(uc >= tk, 1.0, 0.0)).astype(BF16)
    qpos = i * tq + lax.broadcasted_iota(jnp.int32, (tq, tk), 0)
    kcol = lax.broadcasted_iota(jnp.int32, (tq, tk), 1)

    def cond(state):
        return jnp.logical_and(state[0] >= 0, state[1] > 0)

    def step(state):
        j, _, c0, c1, a0, a1 = state
        start = pl.multiple_of(j * tk, tk)
        kt = k_ref[pl.ds(start, tk), :]
        vt = v_ref[pl.ds(start, tk), :]
        kpos = start + kcol
        causal = kpos < qpos
        valid = kpos >= N_PAD
        new = []
        for qh, c, a in ((q_heads[0], c0, a0), (q_heads[1], c1, a1)):
            z = _dot_nt(qh, kt)
            l = jnp.where(causal, jnp.where(valid, -_softplus(z), 0.0), 0.0)
            l_hi = l.astype(BF16)
            l_lo = (l - l_hi.astype(F32)).astype(BF16)
            cs = _dot(l_hi, u) + _dot(l_lo, u)
            e = jnp.exp(z + c + cs[:, :tk])
            w = jnp.where(causal, jnp.where(valid, e, 0.0), 0.0)
            new.append((c + cs[:, tk:], a + _dot(w.astype(BF16), vt)))
        (c0, a0), (c1, a1) = new
        go = (jnp.max(jnp.maximum(c0, c1)) > SB_EXIT).astype(jnp.int32)
        return j - 1, go, c0, c1, a0, a1

    zeros = jnp.zeros((tq, LANES), F32)
    j0 = ((i + 1) * tq - 1) // tk
    state = lax.while_loop(cond, step, (j0, jnp.int32(1), zeros, zeros, zeros, zeros))
    o_ref[...] = jnp.where(lane < SB_HEAD_DIM, state[4], state[5]).astype(o_ref.dtype)


def _sb_attention(q, k, v, tq):
    b, L, _ = q.shape
    npairs = SB_WIDTH // LANES
    return pl.pallas_call(
        functools.partial(_sb_body, tq=tq),
        grid=(b, npairs, L // tq),
        in_specs=[pl.BlockSpec((None, tq, LANES), lambda bi, p, i: (bi, i, p)),
                  pl.BlockSpec((None, L, LANES), lambda bi, p, i: (bi, 0, p)),
                  pl.BlockSpec((None, L, LANES), lambda bi, p, i: (bi, 0, p))],
        out_specs=pl.BlockSpec((None, tq, LANES), lambda bi, p, i: (bi, i, p)),
        out_shape=jax.ShapeDtypeStruct((b, L, SB_WIDTH), BF16),
        compiler_params=_cparams("parallel", "parallel", "parallel"),
        name="sb_attention",
    )(q, k, v)


def _ssd_body(xbc_ref, z_ref, mb_ref, dtt_ref, cw_ref, cb_ref, dtb_ref, dtbc_ref, af_ref, ac_ref,
              df_ref, ng_ref, o_ref, hist_ref, state_ref):
    c = pl.program_id(1)
    T = BLOCK
    GW = SSM_WIDTH // SSM_GROUPS

    @pl.when(c == 0)
    def _():
        hist_ref[0:HIST, :] = jnp.zeros((HIST, CONV_CH), F32)
        state_ref[...] = jnp.zeros_like(state_ref)

    row = c * T + lax.broadcasted_iota(jnp.int32, (T, 1), 0)
    valid_r = row >= N_PAD
    hist_ref[HIST:HIST + T, :] = jnp.where(valid_r, xbc_ref[...], 0.0)
    conv = cb_ref[...]
    for kk in range(CONV_K):
        off = HIST - (CONV_K - 1) + kk
        conv = conv + cw_ref[kk:kk + 1, :] * hist_ref[off:off + T, :]
    tail = hist_ref[T:T + HIST, :]
    hist_ref[0:HIST, :] = tail
    xc = _silu(conv)
    xs = xc[:, :SSM_WIDTH]
    bm = xc[:, SSM_WIDTH:SSM_WIDTH + SSM_GROUPS * SSM_STATE]
    cm = xc[:, SSM_WIDTH + SSM_GROUPS * SSM_STATE:]

    er = lax.broadcasted_iota(jnp.int32, (LANES, SSM_WIDTH), 0)
    ec = lax.broadcasted_iota(jnp.int32, (LANES, SSM_WIDTH), 1)
    expand = jnp.where(er - DT_LANE0 == ec // SSM_HEAD_DIM, 1.0, 0.0).astype(BF16)
    dt_raw = _dot_f32_lhs(mb_ref[...], expand)
    dt = jnp.where(valid_r, _softplus(dt_raw + dtb_ref[...]), 0.0)
    tr = lax.broadcasted_iota(jnp.int32, (T, T), 0)
    tc = lax.broadcasted_iota(jnp.int32, (T, T), 1)
    lower = tr >= tc
    tri = jnp.where(lower, 1.0, 0.0).astype(BF16)
    acs = _dot_f32_rhs(tri, dt * af_ref[...])
    acs_last = acs[T - 1:T, :]
    x_dt = xs * dt
    x_dt_bf = x_dt.astype(BF16)
    xd_bf = (x_dt * jnp.exp(acs_last - acs)).astype(BF16)
    exp_acs = jnp.exp(acs)
    chunk_decay = jnp.exp(acs_last)

    colt = c * T + lax.broadcasted_iota(jnp.int32, (1, T), 1)
    dt_t = jnp.where(colt >= N_PAD, _softplus(dtt_ref[...] + dtbc_ref[...]), 0.0)
    tri_t = jnp.where(tr <= tc, 1.0, 0.0).astype(BF16)
    acs_t = _dot_f32_lhs(dt_t * ac_ref[...], tri_t)

    lane = lax.broadcasted_iota(jnp.int32, (T, LANES), 1)
    first = lane < SSM_HEAD_DIM
    y_parts = []
    for g in range(SSM_GROUPS):
        gs = slice(g * GW, (g + 1) * GW)
        bg = bm[:, g * SSM_STATE:(g + 1) * SSM_STATE]
        cg = cm[:, g * SSM_STATE:(g + 1) * SSM_STATE].astype(BF16)
        cb = _dot_nt(cg, bg.astype(BF16))
        prev = state_ref[:, gs]
        y_off = _dot(cg, prev.astype(BF16)) * exp_acs[:, gs]
        state_ref[:, gs] = prev * chunk_decay[:, gs] + _dot(bg.T.astype(BF16), xd_bf[:, gs])
        for pr in range(GW // LANES):
            h0 = (g * GW + pr * LANES) // SSM_HEAD_DIM
            ps = slice(g * GW + pr * LANES, g * GW + (pr + 1) * LANES)
            xp = x_dt_bf[:, ps]
            yd = []
            for h in (h0, h0 + 1):
                seg = acs[:, h * SSM_HEAD_DIM:h * SSM_HEAD_DIM + 1] - acs_t[h:h + 1, :]
                m = cb * jnp.exp(jnp.where(lower, seg, -jnp.inf))
                yd.append(_dot(m.astype(BF16), xp))
            y_diag = jnp.where(first, yd[0], yd[1])
            y_parts.append(y_diag + y_off[:, pr * LANES:(pr + 1) * LANES] + xs[:, ps] * df_ref[:, ps])
    y = jnp.concatenate(y_parts, axis=-1)
    o_ref[...] = _rms(y * _silu(z_ref[...]), ng_ref[...]).astype(o_ref.dtype)


def _ssd(xbc, z, mb, dt_t, cw, cb, dtb_full, dtb_col, a_full, a_col, d_full, ng):
    b, L, _ = xbc.shape
    T = BLOCK
    return pl.pallas_call(
        _ssd_body,
        grid=(b, L // T),
        in_specs=[pl.BlockSpec((None, T, CONV_CH), lambda bi, c: (bi, c, 0)),
                  pl.BlockSpec((None, T, SSM_WIDTH), lambda bi, c: (bi, c, 0)),
                  pl.BlockSpec((None, T, LANES), lambda bi, c: (bi, c, 0)),
                  pl.BlockSpec((None, SSM_HEADS, T), lambda bi, c: (bi, 0, c)),
                  _const_spec((CONV_K, CONV_CH)), _const_spec((1, CONV_CH)),
                  _const_spec((1, SSM_WIDTH)), _const_spec((SSM_HEADS, 1)),
                  _const_spec((1, SSM_WIDTH)), _const_spec((SSM_HEADS, 1)),
                  _const_spec((1, SSM_WIDTH)), _const_spec((1, SSM_WIDTH))],
        out_specs=pl.BlockSpec((None, T, SSM_WIDTH), lambda bi, c: (bi, c, 0)),
        out_shape=jax.ShapeDtypeStruct((b, L, SSM_WIDTH), BF16),
        scratch_shapes=[pltpu.VMEM((HIST + T, CONV_CH), F32),
                        pltpu.VMEM((SSM_STATE, SSM_WIDTH), F32)],
        compiler_params=_cparams("parallel", "arbitrary"),
        name="ssd",
    )(xbc, z, mb, dt_t, cw, cb, dtb_full, dtb_col, a_full, a_col, d_full, ng)


def _mla_prep_body(cq_ref, ckv_ref, ma_ref, mb_ref, gq_ref, gkv_ref, wq_ref, wqr_ref, wk_ref, wv_ref,
                   cosq_ref, sinq_ref, cosk_ref, sink_ref, qx_ref, kx_ref, v_ref):
    scale = 1.0 / math.sqrt(QK_NOPE + QK_ROPE)
    qn = _rms(cq_ref[...], gq_ref[...]).astype(BF16)
    q = _dot(qn, wq_ref[...])
    qr = _dot(qn, wqr_ref[...])
    cosq = cosq_ref[...]
    sinq = sinq_ref[...]
    kvn = _rms(ckv_ref[...], gkv_ref[...]).astype(BF16)
    kn = _dot(kvn, wk_ref[...])
    v_ref[...] = _dot(kvn, wv_ref[...]).astype(v_ref.dtype)
    k_rope = ma_ref[...] * cosk_ref[...] + mb_ref[...] * sink_ref[...]
    for h in range(MLA_HEADS):
        hs = slice(h * LANES, (h + 1) * LANES)
        qx_ref[:, hs] = ((q[:, hs] * cosq + qr[:, hs] * sinq) * scale).astype(qx_ref.dtype)
        kx_ref[:, hs] = (kn[:, hs] + k_rope).astype(kx_ref.dtype)


def _mla_prep(cq, ckv, ma, mb, gq, gkv, wq, wqr, wk, wv, tabs, tm):
    b, L, _ = cq.shape
    HW = MLA_HEADS * LANES
    row = lambda wd: pl.BlockSpec((None, tm, wd), lambda bi, i: (bi, i, 0))
    tab = pl.BlockSpec((tm, LANES), lambda bi, i: (i, 0))
    return pl.pallas_call(
        _mla_prep_body,
        grid=(b, L // tm),
        in_specs=[row(Q_LORA), row(KV_LORA), row(LANES), row(LANES),
                  _const_spec((1, Q_LORA)), _const_spec((1, KV_LORA)),
                  _const_spec((Q_LORA, HW)), _const_spec((Q_LORA, HW)),
                  _const_spec((KV_LORA, HW)), _const_spec((KV_LORA, MLA_WIDTH)),
                  tab, tab, tab, tab],
        out_specs=[row(HW), row(HW), row(MLA_WIDTH)],
        out_shape=[jax.ShapeDtypeStruct((b, L, HW), BF16), jax.ShapeDtypeStruct((b, L, HW), BF16),
                   jax.ShapeDtypeStruct((b, L, MLA_WIDTH), BF16)],
        compiler_params=_cparams("parallel", "parallel"),
        name="mla_prep",
    )(cq, ckv, ma, mb, gq, gkv, wq, wqr, wk, wv, *tabs)


def _mla_body(q_ref, k_ref, v_ref, o_ref, *, tq, tk):
    i = pl.program_id(2)
    n_tiles = ((i + 1) * tq + tk - 1) // tk
    qpos = i * tq + lax.broadcasted_iota(jnp.int32, (tq, tk), 0)
    kcol = lax.broadcasted_iota(jnp.int32, (tq, tk), 1)
    qs = (q_ref[:, :LANES], q_ref[:, LANES:])

    def step(j, carry):
        start = pl.multiple_of(j * tk, tk)
        vt = v_ref[pl.ds(start, tk), :]
        kpos = start + kcol
        causal = kpos <= qpos
        valid = kpos >= N_PAD
        out = []
        for h in range(2):
            m, l, a = carry[3 * h:3 * h + 3]
            kt = k_ref[pl.ds(start, tk), h * LANES:(h + 1) * LANES]
            s = _dot_nt(qs[h], kt)
            s = jnp.where(causal, jnp.where(valid, s, NEG_BIG), NEG_BIG)
            m_new = jnp.maximum(m, jnp.max(s, axis=-1, keepdims=True))
            alpha = jnp.exp(m - m_new)
            p = jnp.exp(s - m_new)
            l = alpha * l + jnp.sum(p, axis=-1, keepdims=True)
            a = alpha * a + _dot(p.astype(BF16), vt)
            out += [m_new, l, a]
        return tuple(out)

    m0 = jnp.full((tq, 1), NEG_BIG, F32)
    l0 = jnp.zeros((tq, 1), F32)
    a0 = jnp.zeros((tq, LANES), F32)
    res = lax.fori_loop(0, n_tiles, step, (m0, l0, a0, m0, l0, a0))
    lane = lax.broadcasted_iota(jnp.int32, (tq, LANES), 1)
    o_ref[...] = jnp.where(lane < V_HEAD, res[2] / res[1], res[5] / res[4]).astype(o_ref.dtype)


def _mla_attention(qx, kx, v, tq, tk):
    b, L, _ = qx.shape
    npairs = MLA_WIDTH // LANES
    return pl.pallas_call(
        functools.partial(_mla_body, tq=tq, tk=tk),
        grid=(b, npairs, L // tq),
        in_specs=[pl.BlockSpec((None, tq, 2 * LANES), lambda bi, p, i: (bi, i, p)),
                  pl.BlockSpec((None, L, 2 * LANES), lambda bi, p, i: (bi, 0, p)),
                  pl.BlockSpec((None, L, LANES), lambda bi, p, i: (bi, 0, p))],
        out_specs=pl.BlockSpec((None, tq, LANES), lambda bi, p, i: (bi, i, p)),
        out_shape=jax.ShapeDtypeStruct((b, L, MLA_WIDTH), BF16),
        compiler_params=_cparams("parallel", "parallel", "parallel"),
        name="mla_attention",
    )(qx, kx, v)


def _mix_ffn_body(h_ref, osb_ref, ossm_ref, omla_ref, wo_ref, g_ref, wg_ref, wu_ref, wd_ref, o_ref, *, ff_chunk):
    h = h_ref[...]
    h = h + _dot(osb_ref[...], wo_ref[0:SB_WIDTH, :])
    h = h + _dot(ossm_ref[...], wo_ref[SB_WIDTH:SB_WIDTH + SSM_WIDTH, :])
    h = h + _dot(omla_ref[...], wo_ref[SB_WIDTH + SSM_WIDTH:, :])
    fn = _rms(h, g_ref[...]).astype(BF16)
    o_ref[...] = h
    for f in range(0, D_FF, ff_chunk):
        gate = _dot(fn, wg_ref[:, f:f + ff_chunk])
        up = _dot(fn, wu_ref[:, f:f + ff_chunk])
        o_ref[...] += _dot((_silu(gate) * up).astype(BF16), wd_ref[f:f + ff_chunk, :])


def _mix_ffn(h, osb, ossm, omla, wo, g, wg, wu, wd, tm, ff_chunk):
    rows = h.shape[0]
    row = lambda wd_: pl.BlockSpec((tm, wd_), lambda i: (i, 0))
    return pl.pallas_call(
        functools.partial(_mix_ffn_body, ff_chunk=ff_chunk),
        grid=(rows // tm,),
        in_specs=[row(D_MODEL), row(SB_WIDTH), row(SSM_WIDTH), row(MLA_WIDTH),
                  _const_spec((MIX_WIDTH, D_MODEL)), _const_spec((1, D_MODEL)),
                  _const_spec((D_MODEL, D_FF)), _const_spec((D_MODEL, D_FF)),
                  _const_spec((D_FF, D_MODEL))],
        out_specs=row(D_MODEL),
        out_shape=jax.ShapeDtypeStruct((rows, D_MODEL), F32),
        compiler_params=_cparams("parallel"),
        name="mix_ffn",
    )(h, osb, ossm, omla, wo, g, wg, wu, wd)


def _final_norm_body(h_ref, g_ref, o_ref):
    o_ref[...] = _rms(h_ref[...], g_ref[...])


def _final_norm(h, g, seq, tm):
    b, L, _ = h.shape
    skip = (L - seq) // tm
    return pl.pallas_call(
        _final_norm_body,
        grid=(b, seq // tm),
        in_specs=[pl.BlockSpec((None, tm, D_MODEL), lambda bi, i: (bi, i + skip, 0)),
                  _const_spec((1, D_MODEL))],
        out_specs=pl.BlockSpec((None, tm, D_MODEL), lambda bi, i: (bi, i, 0)),
        out_shape=jax.ShapeDtypeStruct((b, seq, D_MODEL), F32),
        compiler_params=_cparams("parallel", "parallel"),
        name="final_norm",
    )(h, g)


def _rot_half_cols(w):
    half = QK_ROPE // 2
    return jnp.concatenate([-w[..., half:], w[..., :half]], axis=-1)


def _lane_block(cols, lane0):
    rows, n = cols.shape
    return jnp.pad(cols, ((0, 0), (lane0, LANES - lane0 - n)))


def _prep_in_weight(w_in):
    o = 0
    parts = {}
    for name, wd in (("q", SB_WIDTH), ("k", SB_WIDTH), ("v", SB_WIDTH), ("z", SSM_WIDTH), ("xbc", CONV_CH),
                     ("dt", SSM_HEADS), ("cq", Q_LORA), ("ckv", KV_LORA), ("kr", QK_ROPE)):
        parts[name] = w_in[:, o:o + wd]
        o += wd
    sb_scale = 1.0 / math.sqrt(SB_HEAD_DIM)
    ma = _lane_block(parts["kr"], ROPE_LANE0)
    mb = _lane_block(jnp.concatenate([_rot_half_cols(parts["kr"]), parts["dt"]], axis=-1), ROPE_LANE0)
    return jnp.concatenate([parts["q"] * sb_scale, parts["k"], parts["v"], parts["z"], parts["xbc"],
                            parts["cq"], parts["ckv"], ma, mb], axis=-1).astype(BF16)


def _prep_mla_weights(w_uq, w_ukv):
    dq = QK_NOPE + QK_ROPE
    wq = w_uq.reshape(Q_LORA, MLA_HEADS, dq)
    pad = jnp.zeros((Q_LORA, MLA_HEADS, LANES - dq), F32)
    wq_x = jnp.concatenate([wq, pad], axis=-1)
    wq_r = jnp.concatenate([jnp.zeros((Q_LORA, MLA_HEADS, QK_NOPE), F32), _rot_half_cols(wq[..., QK_NOPE:]), pad],
                           axis=-1)
    wkv = w_ukv.reshape(KV_LORA, MLA_HEADS, QK_NOPE + V_HEAD)
    wk_x = jnp.concatenate([wkv[..., :QK_NOPE], jnp.zeros((KV_LORA, MLA_HEADS, LANES - QK_NOPE), F32)], axis=-1)
    wv = wkv[..., QK_NOPE:]
    flat = lambda a: a.reshape(a.shape[0], -1).astype(BF16)
    return flat(wq_x), flat(wq_r), flat(wk_x), flat(wv)


def _rope_tables(L):
    pos = (jnp.arange(L) - N_PAD).astype(F32)
    inv = ROPE_BASE ** (-jnp.arange(0, QK_ROPE, 2, dtype=F32) / QK_ROPE)
    ang = pos[:, None] * inv[None, :]
    cos2 = jnp.concatenate([jnp.cos(ang), jnp.cos(ang)], axis=-1)
    sin2 = jnp.concatenate([jnp.sin(ang), jnp.sin(ang)], axis=-1)
    cosk = _lane_block(cos2, ROPE_LANE0)
    sink = _lane_block(sin2, ROPE_LANE0)
    cosq = cosk.at[:, :QK_NOPE].set(1.0)
    return cosq, sink, cosk, sink


def _per_head_lanes(p):
    return jnp.repeat(p, SSM_HEAD_DIM)[None, :]


def kernel(x, meta_tokens, mix_norm_g, w_in, conv_w, conv_b, dt_bias, a_log, d_skip, ssm_norm_g, q_norm_g, w_uq,
           kv_norm_g, w_ukv, w_out, ffn_norm_g, w_gate, w_up, w_down, final_norm_g):
    b, seq, _ = x.shape
    depth = w_in.shape[0]
    L = seq + BLOCK
    rows = b * L
    tm = 640 if rows % 640 == 0 else BLOCK
    tm_b = 640 if L % 640 == 0 else BLOCK
    h = jnp.concatenate([jnp.zeros((b, N_PAD, D_MODEL), F32),
                         jnp.broadcast_to(meta_tokens[None], (b, N_META, D_MODEL)), x], axis=1).reshape(rows, D_MODEL)
    tabs = _rope_tables(L)
    r3 = lambda a: a.reshape(b, L, a.shape[-1])
    for i in range(depth):
        w_in_p = _prep_in_weight(w_in[i])
        q, k, v, z, xbc, cq, ckv, ma, mb = _inproj(h, mix_norm_g[i][None, :], w_in_p, tm)
        o_sb = _sb_attention(r3(q), r3(k), r3(v), BLOCK)
        a = -jnp.exp(a_log[i])
        mb3 = r3(mb)
        dt_t = jnp.swapaxes(mb3[:, :, DT_LANE0:DT_LANE0 + SSM_HEADS], 1, 2)
        o_ssm = _ssd(r3(xbc), r3(z), mb3, dt_t, conv_w[i], conv_b[i][None, :],
                     _per_head_lanes(dt_bias[i]), dt_bias[i][:, None], _per_head_lanes(a), a[:, None],
                     _per_head_lanes(d_skip[i]), ssm_norm_g[i][None, :])
        wq_x, wq_r, wk_x, wv = _prep_mla_weights(w_uq[i], w_ukv[i])
        qx, kx, vm = _mla_prep(r3(cq), r3(ckv), r3(ma), mb3, q_norm_g[i][None, :], kv_norm_g[i][None, :],
                               wq_x, wq_r, wk_x, wv, tabs, tm_b)
        o_mla = _mla_attention(qx, kx, vm, BLOCK, BLOCK)
        h = _mix_ffn(h, o_sb.reshape(rows, SB_WIDTH), o_ssm.reshape(rows, SSM_WIDTH), o_mla.reshape(rows, MLA_WIDTH),
                     w_out[i].astype(BF16), ffn_norm_g[i][None, :], w_gate[i].astype(BF16), w_up[i].astype(BF16),
                     w_down[i].astype(BF16), tm, 704)
    return _final_norm(h.reshape(b, L, D_MODEL), final_norm_g[None, :], seq, BLOCK)
```

```python
import functools
import math

import jax
import jax.numpy as jnp
from jax import lax
from jax.experimental import pallas as pl
from jax.experimental.pallas import tpu as pltpu

F32 = jnp.float32
BF16 = jnp.bfloat16

D_MODEL = 1024
N_META = 16
BLOCK = 128
N_PAD = BLOCK - N_META
SB_HEADS = 8
SB_HEAD_DIM = 64
SB_WIDTH = SB_HEADS * SB_HEAD_DIM
SSM_HEADS = 16
SSM_HEAD_DIM = 64
SSM_WIDTH = SSM_HEADS * SSM_HEAD_DIM
SSM_GROUPS = 2
SSM_STATE = 128
CONV_K = 4
CONV_CH = SSM_WIDTH + 2 * SSM_GROUPS * SSM_STATE
MLA_HEADS = 8
Q_LORA = 384
KV_LORA = 256
QK_NOPE = 64
QK_ROPE = 32
V_HEAD = 64
MLA_WIDTH = MLA_HEADS * V_HEAD
ROPE_BASE = 10000.0
MIX_WIDTH = SB_WIDTH + SSM_WIDTH + MLA_WIDTH
D_FF = 2816
EPS = 1e-6

LANES = 128
ROPE_LANE0 = QK_NOPE
DT_LANE0 = QK_NOPE + QK_ROPE
HIST = 8
SB_EXIT = -110.0
NEG_BIG = -1e30
VMEM_LIMIT = 56 * 1024 * 1024

IN_SEGS = (("q", SB_WIDTH, BF16), ("k", SB_WIDTH, BF16), ("v", SB_WIDTH, BF16),
           ("z", SSM_WIDTH, F32), ("xbc", CONV_CH, F32), ("cq", Q_LORA, F32),
           ("ckv", KV_LORA, F32), ("ma", LANES, F32), ("mb", LANES, F32))
IN_WIDTH = sum(s[1] for s in IN_SEGS)


def _cparams(*sem):
    return pltpu.CompilerParams(dimension_semantics=sem, vmem_limit_bytes=VMEM_LIMIT)


def _const_spec(shape):
    nd = len(shape)
    return pl.BlockSpec(shape, lambda *_: (0,) * nd, pipeline_mode=pl.Buffered(1))


def _rms(x, g):
    return x * lax.rsqrt(jnp.mean(x * x, axis=-1, keepdims=True) + EPS) * g


def _silu(x):
    return x * (1.0 / (1.0 + jnp.exp(-x)))


def _softplus(x):
    return jnp.maximum(x, 0.0) + jnp.log(1.0 + jnp.exp(-jnp.abs(x)))


def _split3(x):
    hi = x.astype(BF16)
    r1 = x - hi.astype(F32)
    mid = r1.astype(BF16)
    lo = (r1 - mid.astype(F32)).astype(BF16)
    return hi, mid, lo


def _dot(a, b):
    return jnp.dot(a, b, preferred_element_type=F32)


def _dot_nt(a, b):
    return lax.dot_general(a, b, (((1,), (1,)), ((), ())), preferred_element_type=F32)


def _dot_f32_lhs(x, rhs01):
    return sum(_dot(p, rhs01) for p in _split3(x))


def _dot_f32_rhs(lhs01, x):
    return sum(_dot(lhs01, p) for p in _split3(x))


def _inproj_body(h_ref, g_ref, w_ref, *out_refs):
    hn = _rms(h_ref[...], g_ref[...]).astype(BF16)
    off = 0
    for ref in out_refs:
        width = ref.shape[-1]
        ref[...] = _dot(hn, w_ref[:, off:off + width]).astype(ref.dtype)
        off += width


def _inproj(h, g, w, tm):
    rows = h.shape[0]
    return pl.pallas_call(
        _inproj_body,
        grid=(rows // tm,),
        in_specs=[pl.BlockSpec((tm, D_MODEL), lambda i: (i, 0)),
                  _const_spec((1, D_MODEL)),
                  _const_spec((D_MODEL, IN_WIDTH))],
        out_specs=[pl.BlockSpec((tm, wd), lambda i: (i, 0)) for _, wd, _ in IN_SEGS],
        out_shape=[jax.ShapeDtypeStruct((rows, wd), dt) for _, wd, dt in IN_SEGS],
        compiler_params=_cparams("parallel"),
        name="inproj",
    )(h, g, w)


def _sb_body(q_ref, k_ref, v_ref, o_ref):
    T = BLOCK
    NPAIR = SB_WIDTH // LANES
    NH = SB_HEADS
    i = pl.program_id(1)
    first = lax.broadcasted_iota(jnp.int32, (T, LANES), 1) < SB_HEAD_DIM
    q_rows = []
    for p in range(NPAIR):
        qp = q_ref[:, p * LANES:(p + 1) * LANES]
        qz = jnp.zeros_like(qp)
        q_rows.append(jnp.concatenate([jnp.where(first, qp, qz), jnp.where(first, qz, qp)], axis=0))
    ur = lax.broadcasted_iota(jnp.int32, (T, 2 * T), 0)
    uc = lax.broadcasted_iota(jnp.int32, (T, 2 * T), 1)
    u = jnp.where(ur >= uc, 1.0, jnp.where(uc >= T, 1.0, 0.0)).astype(BF16)

    def tile(j, c, accs, masked):
        start = pl.multiple_of(j * T, T)
        z = jnp.concatenate([_dot_nt(q_rows[p], k_ref[pl.ds(start, T), p * LANES:(p + 1) * LANES])
                             for p in range(NPAIR)], axis=0)
        l = -_softplus(z)
        if masked:
            qpos = i * T + (lax.broadcasted_iota(jnp.int32, (NH * T, T), 0) & (T - 1))
            kpos = start + lax.broadcasted_iota(jnp.int32, (NH * T, T), 1)
            keep = lambda x: jnp.where(kpos < qpos, jnp.where(kpos >= N_PAD, x, 0.0), 0.0)
            l = keep(l)
        l_hi = l.astype(BF16)
        l_lo = (l - l_hi.astype(F32)).astype(BF16)
        cs = _dot(l_hi, u) + _dot(l_lo, u)
        w = jnp.exp(z + c + cs[:, :T])
        if masked:
            w = keep(w)
        w = w.astype(BF16)
        new_accs = []
        for p in range(NPAIR):
            vp = v_ref[pl.ds(start, T), p * LANES:(p + 1) * LANES]
            vz = jnp.zeros_like(vp)
            v_stack = jnp.concatenate([jnp.where(first, vp, vz), jnp.where(first, vz, vp)], axis=0)
            w_pair = jnp.concatenate([w[2 * p * T:(2 * p + 1) * T], w[(2 * p + 1) * T:(2 * p + 2) * T]], axis=1)
            new_accs.append(accs[p] + _dot(w_pair, v_stack))
        c = c + cs[:, T:]
        go = (jnp.max(c) > SB_EXIT).astype(jnp.int32)
        return go, c, new_accs

    def stepper(masked):
        def body(state):
            go, c, accs = tile(state[0], state[2], list(state[3:]), masked)
            return (state[0] - 1, go, c, *accs)
        return body

    c0 = jnp.zeros((NH * T, T), F32)
    acc0 = [jnp.zeros((T, LANES), F32)] * NPAIR
    go, c, accs = tile(i, c0, acc0, True)
    state = (i - 1, go, c, *accs)
    state = lax.while_loop(lambda s: jnp.logical_and(s[0] >= 1, s[1] > 0), stepper(False), state)
    state = lax.while_loop(lambda s: jnp.logical_and(s[0] == 0, s[1] > 0), stepper(True), state)
    for p in range(NPAIR):
        o_ref[:, p * LANES:(p + 1) * LANES] = state[3 + p].astype(o_ref.dtype)


def _sb_attention(q, k, v):
    b, L, _ = q.shape
    T = BLOCK
    return pl.pallas_call(
        _sb_body,
        grid=(b, L // T),
        in_specs=[pl.BlockSpec((None, T, SB_WIDTH), lambda bi, i: (bi, i, 0)),
                  pl.BlockSpec((None, L, SB_WIDTH), lambda bi, i: (bi, 0, 0)),
                  pl.BlockSpec((None, L, SB_WIDTH), lambda bi, i: (bi, 0, 0))],
        out_specs=pl.BlockSpec((None, T, SB_WIDTH), lambda bi, i: (bi, i, 0)),
        out_shape=jax.ShapeDtypeStruct((b, L, SB_WIDTH), BF16),
        compiler_params=_cparams("parallel", "parallel"),
        name="sb_attention",
    )(q, k, v)


def _ssd_body(xbc_ref, z_ref, mb_ref, dtt_ref, cw_ref, cb_ref, dtb_ref, dtbc_ref, af_ref, ac_ref,
              df_ref, ng_ref, o_ref, hist_ref, state_ref):
    c = pl.program_id(1)
    T = BLOCK
    GW = SSM_WIDTH // SSM_GROUPS

    @pl.when(c == 0)
    def _():
        hist_ref[0:HIST, :] = jnp.zeros((HIST, CONV_CH), F32)
        state_ref[...] = jnp.zeros_like(state_ref)

    row = c * T + lax.broadcasted_iota(jnp.int32, (T, 1), 0)
    valid_r = row >= N_PAD
    hist_ref[HIST:HIST + T, :] = jnp.where(valid_r, xbc_ref[...], 0.0)
    conv = cb_ref[...]
    for kk in range(CONV_K):
        off = HIST - (CONV_K - 1) + kk
        conv = conv + cw_ref[kk:kk + 1, :] * hist_ref[off:off + T, :]
    tail = hist_ref[T:T + HIST, :]
    hist_ref[0:HIST, :] = tail
    xc = _silu(conv)
    xs = xc[:, :SSM_WIDTH]
    bm = xc[:, SSM_WIDTH:SSM_WIDTH + SSM_GROUPS * SSM_STATE]
    cm = xc[:, SSM_WIDTH + SSM_GROUPS * SSM_STATE:]

    er = lax.broadcasted_iota(jnp.int32, (LANES, SSM_WIDTH), 0)
    ec = lax.broadcasted_iota(jnp.int32, (LANES, SSM_WIDTH), 1)
    expand = jnp.where(er - DT_LANE0 == ec // SSM_HEAD_DIM, 1.0, 0.0).astype(BF16)
    dt_raw = _dot_f32_lhs(mb_ref[...], expand)
    dt = jnp.where(valid_r, _softplus(dt_raw + dtb_ref[...]), 0.0)
    tr = lax.broadcasted_iota(jnp.int32, (T, T), 0)
    tc = lax.broadcasted_iota(jnp.int32, (T, T), 1)
    lower = tr >= tc
    tri = jnp.where(lower, 1.0, 0.0).astype(BF16)
    acs = _dot_f32_rhs(tri, dt * af_ref[...])
    acs_last = acs[T - 1:T, :]
    x_dt = xs * dt
    x_dt_bf = x_dt.astype(BF16)
    xd_bf = (x_dt * jnp.exp(acs_last - acs)).astype(BF16)
    exp_acs = jnp.exp(acs)
    chunk_decay = jnp.exp(acs_last)

    colt = c * T + lax.broadcasted_iota(jnp.int32, (1, T), 1)
    dt_t = jnp.where(colt >= N_PAD, _softplus(dtt_ref[...] + dtbc_ref[...]), 0.0)
    tri_t = jnp.where(tr <= tc, 1.0, 0.0).astype(BF16)
    acs_t = _dot_f32_lhs(dt_t * ac_ref[...], tri_t)

    lane = lax.broadcasted_iota(jnp.int32, (T, LANES), 1)
    first = lane < SSM_HEAD_DIM
    y_parts = []
    for g in range(SSM_GROUPS):
        gs = slice(g * GW, (g + 1) * GW)
        bg = bm[:, g * SSM_STATE:(g + 1) * SSM_STATE]
        cg = cm[:, g * SSM_STATE:(g + 1) * SSM_STATE].astype(BF16)
        cb = _dot_nt(cg, bg.astype(BF16))
        prev = state_ref[:, gs]
        y_off = _dot(cg, prev.astype(BF16)) * exp_acs[:, gs]
        state_ref[:, gs] = prev * chunk_decay[:, gs] + _dot(bg.T.astype(BF16), xd_bf[:, gs])
        for pr in range(GW // LANES):
            h0 = (g * GW + pr * LANES) // SSM_HEAD_DIM
            ps = slice(g * GW + pr * LANES, g * GW + (pr + 1) * LANES)
            xp = x_dt_bf[:, ps]
            yd = []
            for h in (h0, h0 + 1):
                seg = acs[:, h * SSM_HEAD_DIM:h * SSM_HEAD_DIM + 1] - acs_t[h:h + 1, :]
                m = cb * jnp.exp(jnp.where(lower, seg, -jnp.inf))
                yd.append(_dot(m.astype(BF16), xp))
            y_diag = jnp.where(first, yd[0], yd[1])
            y_parts.append(y_diag + y_off[:, pr * LANES:(pr + 1) * LANES] + xs[:, ps] * df_ref[:, ps])
    y = jnp.concatenate(y_parts, axis=-1)
    o_ref[...] = _rms(y * _silu(z_ref[...]), ng_ref[...]).astype(o_ref.dtype)


def _ssd(xbc, z, mb, dt_t, cw, cb, dtb_full, dtb_col, a_full, a_col, d_full, ng):
    b, L, _ = xbc.shape
    T = BLOCK
    return pl.pallas_call(
        _ssd_body,
        grid=(b, L // T),
        in_specs=[pl.BlockSpec((None, T, CONV_CH), lambda bi, c: (bi, c, 0)),
                  pl.BlockSpec((None, T, SSM_WIDTH), lambda bi, c: (bi, c, 0)),
                  pl.BlockSpec((None, T, LANES), lambda bi, c: (bi, c, 0)),
                  pl.BlockSpec((None, SSM_HEADS, T), lambda bi, c: (bi, 0, c)),
                  _const_spec((CONV_K, CONV_CH)), _const_spec((1, CONV_CH)),
                  _const_spec((1, SSM_WIDTH)), _const_spec((SSM_HEADS, 1)),
                  _const_spec((1, SSM_WIDTH)), _const_spec((SSM_HEADS, 1)),
                  _const_spec((1, SSM_WIDTH)), _const_spec((1, SSM_WIDTH))],
        out_specs=pl.BlockSpec((None, T, SSM_WIDTH), lambda bi, c: (bi, c, 0)),
        out_shape=jax.ShapeDtypeStruct((b, L, SSM_WIDTH), BF16),
        scratch_shapes=[pltpu.VMEM((HIST + T, CONV_CH), F32),
                        pltpu.VMEM((SSM_STATE, SSM_WIDTH), F32)],
        compiler_params=_cparams("parallel", "arbitrary"),
        name="ssd",
    )(xbc, z, mb, dt_t, cw, cb, dtb_full, dtb_col, a_full, a_col, d_full, ng)


def _mla_prep_body(cq_ref, ckv_ref, ma_ref, mb_ref, gq_ref, gkv_ref, wq_ref, wqr_ref, wk_ref, wv_ref,
                   cosq_ref, sinq_ref, cosk_ref, sink_ref, qx_ref, kx_ref, v_ref):
    scale = 1.0 / math.sqrt(QK_NOPE + QK_ROPE)
    qn = _rms(cq_ref[...], gq_ref[...]).astype(BF16)
    q = _dot(qn, wq_ref[...])
    qr = _dot(qn, wqr_ref[...])
    cosq = cosq_ref[...]
    sinq = sinq_ref[...]
    kvn = _rms(ckv_ref[...], gkv_ref[...]).astype(BF16)
    kn = _dot(kvn, wk_ref[...])
    v_ref[...] = _dot(kvn, wv_ref[...]).astype(v_ref.dtype)
    k_rope = ma_ref[...] * cosk_ref[...] + mb_ref[...] * sink_ref[...]
    for h in range(MLA_HEADS):
        hs = slice(h * LANES, (h + 1) * LANES)
        qx_ref[:, hs] = ((q[:, hs] * cosq + qr[:, hs] * sinq) * scale).astype(qx_ref.dtype)
        kx_ref[:, hs] = (kn[:, hs] + k_rope).astype(kx_ref.dtype)


def _mla_prep(cq, ckv, ma, mb, gq, gkv, wq, wqr, wk, wv, tabs, tm):
    b, L, _ = cq.shape
    HW = MLA_HEADS * LANES
    row = lambda wd: pl.BlockSpec((None, tm, wd), lambda bi, i: (bi, i, 0))
    tab = pl.BlockSpec((tm, LANES), lambda bi, i: (i, 0))
    return pl.pallas_call(
        _mla_prep_body,
        grid=(b, L // tm),
        in_specs=[row(Q_LORA), row(KV_LORA), row(LANES), row(LANES),
                  _const_spec((1, Q_LORA)), _const_spec((1, KV_LORA)),
                  _const_spec((Q_LORA, HW)), _const_spec((Q_LORA, HW)),
                  _const_spec((KV_LORA, HW)), _const_spec((KV_LORA, MLA_WIDTH)),
                  tab, tab, tab, tab],
        out_specs=[row(HW), row(HW), row(MLA_WIDTH)],
        out_shape=[jax.ShapeDtypeStruct((b, L, HW), BF16), jax.ShapeDtypeStruct((b, L, HW), BF16),
                   jax.ShapeDtypeStruct((b, L, MLA_WIDTH), BF16)],
        compiler_params=_cparams("parallel", "parallel"),
        name="mla_prep",
    )(cq, ckv, ma, mb, gq, gkv, wq, wqr, wk, wv, *tabs)


def _mla_body(q_ref, k_ref, v_ref, o_ref, *, tq, tk):
    i = pl.program_id(2)
    n_diag = tq // tk
    qs = (q_ref[:, :LANES], q_ref[:, LANES:])
    first_v = lax.broadcasted_iota(jnp.int32, (tk, LANES), 1) < V_HEAD
    first_o = lax.broadcasted_iota(jnp.int32, (tq, LANES), 1) < V_HEAD

    def tile(j, carry, masked):
        start = pl.multiple_of(j * tk, tk)
        vt = v_ref[pl.ds(start, tk), :]
        vz = jnp.zeros_like(vt)
        v_stack = jnp.concatenate([jnp.where(first_v, vt, vz), jnp.where(first_v, vz, vt)], axis=0)
        if masked:
            qpos = i * tq + lax.broadcasted_iota(jnp.int32, (tq, tk), 0)
            kpos = start + lax.broadcasted_iota(jnp.int32, (tq, tk), 1)
        stats, alphas, ps = [], [], []
        for h in range(2):
            m, l = carry[2 * h:2 * h + 2]
            s = _dot_nt(qs[h], k_ref[pl.ds(start, tk), h * LANES:(h + 1) * LANES])
            if masked:
                s = jnp.where(kpos <= qpos, jnp.where(kpos >= N_PAD, s, NEG_BIG), NEG_BIG)
            m_new = jnp.maximum(m, jnp.max(s, axis=-1, keepdims=True))
            alpha = jnp.exp(m - m_new)
            p = jnp.exp(s - m_new)
            stats += [m_new, alpha * l + jnp.sum(p, axis=-1, keepdims=True)]
            alphas.append(alpha)
            ps.append(p.astype(BF16))
        acc = jnp.where(first_o, alphas[0], alphas[1]) * carry[4] + _dot(jnp.concatenate(ps, axis=1), v_stack)
        return (*stats, acc)

    m0 = jnp.full((tq, 1), NEG_BIG, F32)
    l0 = jnp.zeros((tq, 1), F32)
    carry = tile(0, (m0, l0, m0, l0, jnp.zeros((tq, LANES), F32)), True)
    carry = lax.fori_loop(1, i * n_diag, lambda j, cr: tile(j, cr, False), carry)
    carry = lax.fori_loop(jnp.maximum(i * n_diag, 1), (i + 1) * n_diag, lambda j, cr: tile(j, cr, True), carry)
    o_ref[...] = (carry[4] / jnp.where(first_o, carry[1], carry[3])).astype(o_ref.dtype)


def _mla_attention(qx, kx, v, tq, tk):
    b, L, _ = qx.shape
    npairs = MLA_WIDTH // LANES
    return pl.pallas_call(
        functools.partial(_mla_body, tq=tq, tk=tk),
        grid=(b, npairs, L // tq),
        in_specs=[pl.BlockSpec((None, tq, 2 * LANES), lambda bi, p, i: (bi, i, p)),
                  pl.BlockSpec((None, L, 2 * LANES), lambda bi, p, i: (bi, 0, p)),
                  pl.BlockSpec((None, L, LANES), lambda bi, p, i: (bi, 0, p))],
        out_specs=pl.BlockSpec((None, tq, LANES), lambda bi, p, i: (bi, i, p)),
        out_shape=jax.ShapeDtypeStruct((b, L, MLA_WIDTH), BF16),
        compiler_params=_cparams("parallel", "parallel", "parallel"),
        name="mla_attention",
    )(qx, kx, v)


def _mix_ffn_body(h_ref, osb_ref, ossm_ref, omla_ref, wo_ref, g_ref, wg_ref, wu_ref, wd_ref, o_ref, *, ff_chunk):
    h = h_ref[...]
    h = h + _dot(osb_ref[...], wo_ref[0:SB_WIDTH, :])
    h = h + _dot(ossm_ref[...], wo_ref[SB_WIDTH:SB_WIDTH + SSM_WIDTH, :])
    h = h + _dot(omla_ref[...], wo_ref[SB_WIDTH + SSM_WIDTH:, :])
    fn = _rms(h, g_ref[...]).astype(BF16)
    o_ref[...] = h
    for f in range(0, D_FF, ff_chunk):
        gate = _dot(fn, wg_ref[:, f:f + ff_chunk])
        up = _dot(fn, wu_ref[:, f:f + ff_chunk])
        o_ref[...] += _dot((_silu(gate) * up).astype(BF16), wd_ref[f:f + ff_chunk, :])


def _mix_ffn(h, osb, ossm, omla, wo, g, wg, wu, wd, tm, ff_chunk):
    rows = h.shape[0]
    row = lambda wd_: pl.BlockSpec((tm, wd_), lambda i: (i, 0))
    return pl.pallas_call(
        functools.partial(_mix_ffn_body, ff_chunk=ff_chunk),
        grid=(rows // tm,),
        in_specs=[row(D_MODEL), row(SB_WIDTH), row(SSM_WIDTH), row(MLA_WIDTH),
                  _const_spec((MIX_WIDTH, D_MODEL)), _const_spec((1, D_MODEL)),
                  _const_spec((D_MODEL, D_FF)), _const_spec((D_MODEL, D_FF)),
                  _const_spec((D_FF, D_MODEL))],
        out_specs=row(D_MODEL),
        out_shape=jax.ShapeDtypeStruct((rows, D_MODEL), F32),
        compiler_params=_cparams("parallel"),
        name="mix_ffn",
    )(h, osb, ossm, omla, wo, g, wg, wu, wd)


def _final_norm_body(h_ref, g_ref, o_ref):
    o_ref[...] = _rms(h_ref[...], g_ref[...])


def _final_norm(h, g, seq, tm):
    b, L, _ = h.shape
    skip = (L - seq) // tm
    return pl.pallas_call(
        _final_norm_body,
        grid=(b, seq // tm),
        in_specs=[pl.BlockSpec((None, tm, D_MODEL), lambda bi, i: (bi, i + skip, 0)),
                  _const_spec((1, D_MODEL))],
        out_specs=pl.BlockSpec((None, tm, D_MODEL), lambda bi, i: (bi, i, 0)),
        out_shape=jax.ShapeDtypeStruct((b, seq, D_MODEL), F32),
        compiler_params=_cparams("parallel", "parallel"),
        name="final_norm",
    )(h, g)


def _rot_half_cols(w):
    half = QK_ROPE // 2
    return jnp.concatenate([-w[..., half:], w[..., :half]], axis=-1)


def _lane_block(cols, lane0):
    rows, n = cols.shape
    return jnp.pad(cols, ((0, 0), (lane0, LANES - lane0 - n)))


def _prep_in_weight(w_in):
    o = 0
    parts = {}
    for name, wd in (("q", SB_WIDTH), ("k", SB_WIDTH), ("v", SB_WIDTH), ("z", SSM_WIDTH), ("xbc", CONV_CH),
                     ("dt", SSM_HEADS), ("cq", Q_LORA), ("ckv", KV_LORA), ("kr", QK_ROPE)):
        parts[name] = w_in[:, o:o + wd]
        o += wd
    sb_scale = 1.0 / math.sqrt(SB_HEAD_DIM)
    ma = _lane_block(parts["kr"], ROPE_LANE0)
    mb = _lane_block(jnp.concatenate([_rot_half_cols(parts["kr"]), parts["dt"]], axis=-1), ROPE_LANE0)
    return jnp.concatenate([parts["q"] * sb_scale, parts["k"], parts["v"], parts["z"], parts["xbc"],
                            parts["cq"], parts["ckv"], ma, mb], axis=-1).astype(BF16)


def _prep_mla_weights(w_uq, w_ukv):
    dq = QK_NOPE + QK_ROPE
    wq = w_uq.reshape(Q_LORA, MLA_HEADS, dq)
    pad = jnp.zeros((Q_LORA, MLA_HEADS, LANES - dq), F32)
    wq_x = jnp.concatenate([wq, pad], axis=-1)
    wq_r = jnp.concatenate([jnp.zeros((Q_LORA, MLA_HEADS, QK_NOPE), F32), _rot_half_cols(wq[..., QK_NOPE:]), pad],
                           axis=-1)
    wkv = w_ukv.reshape(KV_LORA, MLA_HEADS, QK_NOPE + V_HEAD)
    wk_x = jnp.concatenate([wkv[..., :QK_NOPE], jnp.zeros((KV_LORA, MLA_HEADS, LANES - QK_NOPE), F32)], axis=-1)
    wv = wkv[..., QK_NOPE:]
    flat = lambda a: a.reshape(a.shape[0], -1).astype(BF16)
    return flat(wq_x), flat(wq_r), flat(wk_x), flat(wv)


def _rope_tables(L):
    pos = (jnp.arange(L) - N_PAD).astype(F32)
    inv = ROPE_BASE ** (-jnp.arange(0, QK_ROPE, 2, dtype=F32) / QK_ROPE)
    ang = pos[:, None] * inv[None, :]
    cos2 = jnp.concatenate([jnp.cos(ang), jnp.cos(ang)], axis=-1)
    sin2 = jnp.concatenate([jnp.sin(ang), jnp.sin(ang)], axis=-1)
    cosk = _lane_block(cos2, ROPE_LANE0)
    sink = _lane_block(sin2, ROPE_LANE0)
    cosq = cosk.at[:, :QK_NOPE].set(1.0)
    return cosq, sink, cosk, sink


def _per_head_lanes(p):
    return jnp.repeat(p, SSM_HEAD_DIM)[None, :]


def kernel(x, meta_tokens, mix_norm_g, w_in, conv_w, conv_b, dt_bias, a_log, d_skip, ssm_norm_g, q_norm_g, w_uq,
           kv_norm_g, w_ukv, w_out, ffn_norm_g, w_gate, w_up, w_down, final_norm_g):
    b, seq, _ = x.shape
    depth = w_in.shape[0]
    L = seq + BLOCK
    rows = b * L
    tm = 640 if rows % 640 == 0 else BLOCK
    tm_b = 640 if L % 640 == 0 else BLOCK
    h = jnp.concatenate([jnp.zeros((b, N_PAD, D_MODEL), F32),
                         jnp.broadcast_to(meta_tokens[None], (b, N_META, D_MODEL)), x], axis=1).reshape(rows, D_MODEL)
    tabs = _rope_tables(L)
    r3 = lambda a: a.reshape(b, L, a.shape[-1])
    for i in range(depth):
        w_in_p = _prep_in_weight(w_in[i])
        q, k, v, z, xbc, cq, ckv, ma, mb = _inproj(h, mix_norm_g[i][None, :], w_in_p, tm)
        o_sb = _sb_attention(r3(q), r3(k), r3(v))
        a = -jnp.exp(a_log[i])
        mb3 = r3(mb)
        dt_t = jnp.swapaxes(mb3[:, :, DT_LANE0:DT_LANE0 + SSM_HEADS], 1, 2)
        o_ssm = _ssd(r3(xbc), r3(z), mb3, dt_t, conv_w[i], conv_b[i][None, :],
                     _per_head_lanes(dt_bias[i]), dt_bias[i][:, None], _per_head_lanes(a), a[:, None],
                     _per_head_lanes(d_skip[i]), ssm_norm_g[i][None, :])
        wq_x, wq_r, wk_x, wv = _prep_mla_weights(w_uq[i], w_ukv[i])
        qx, kx, vm = _mla_prep(r3(cq), r3(ckv), r3(ma), mb3, q_norm_g[i][None, :], kv_norm_g[i][None, :],
                               wq_x, wq_r, wk_x, wv, tabs, tm_b)
        o_mla = _mla_attention(qx, kx, vm, tm_b, tm_b)
        h = _mix_ffn(h, o_sb.reshape(rows, SB_WIDTH), o_ssm.reshape(rows, SSM_WIDTH), o_mla.reshape(rows, MLA_WIDTH),
                     w_out[i].astype(BF16), ffn_norm_g[i][None, :], w_gate[i].astype(BF16), w_up[i].astype(BF16),
                     w_down[i].astype(BF16), tm, 704)
    return _final_norm(h.reshape(b, L, D_MODEL), final_norm_g[None, :], seq, BLOCK)
```

```python
import functools
import math

import jax
import jax.numpy as jnp
from jax import lax
from jax.experimental import pallas as pl
from jax.experimental.pallas import tpu as pltpu

F32 = jnp.float32
BF16 = jnp.bfloat16

D_MODEL = 1024
N_META = 16
BLOCK = 128
N_PAD = BLOCK - N_META
SB_HEADS = 8
SB_HEAD_DIM = 64
SB_WIDTH = SB_HEADS * SB_HEAD_DIM
SSM_HEADS = 16
SSM_HEAD_DIM = 64
SSM_WIDTH = SSM_HEADS * SSM_HEAD_DIM
SSM_GROUPS = 2
SSM_STATE = 128
CONV_K = 4
CONV_CH = SSM_WIDTH + 2 * SSM_GROUPS * SSM_STATE
MLA_HEADS = 8
Q_LORA = 384
KV_LORA = 256
QK_NOPE = 64
QK_ROPE = 32
V_HEAD = 64
MLA_WIDTH = MLA_HEADS * V_HEAD
ROPE_BASE = 10000.0
MIX_WIDTH = SB_WIDTH + SSM_WIDTH + MLA_WIDTH
D_FF = 2816
EPS = 1e-6

LANES = 128
ROPE_LANE0 = QK_NOPE
DT_LANE0 = QK_NOPE + QK_ROPE
HIST = 8
SB_EXIT = -110.0
NEG_BIG = -1e30
MLA_ROWS = 160
VMEM_LIMIT = 56 * 1024 * 1024

IN_SEGS = (("q", SB_WIDTH, BF16), ("k", SB_WIDTH, BF16), ("v", SB_WIDTH, BF16),
           ("z", SSM_WIDTH, F32), ("xbc", CONV_CH, F32), ("cq", Q_LORA, F32),
           ("ckv", KV_LORA, F32), ("ma", LANES, F32), ("mb", LANES, F32))
IN_WIDTH = sum(s[1] for s in IN_SEGS)


def _cparams(*sem):
    return pltpu.CompilerParams(dimension_semantics=sem, vmem_limit_bytes=VMEM_LIMIT)


def _const_spec(shape):
    nd = len(shape)
    return pl.BlockSpec(shape, lambda *_: (0,) * nd, pipeline_mode=pl.Buffered(1))


def _rms(x, g):
    return x * lax.rsqrt(jnp.mean(x * x, axis=-1, keepdims=True) + EPS) * g


def _silu(x):
    return x * (1.0 / (1.0 + jnp.exp(-x)))


def _softplus(x):
    return jnp.maximum(x, 0.0) + jnp.log(1.0 + jnp.exp(-jnp.abs(x)))


def _split3(x):
    hi = x.astype(BF16)
    r1 = x - hi.astype(F32)
    mid = r1.astype(BF16)
    lo = (r1 - mid.astype(F32)).astype(BF16)
    return hi, mid, lo


def _dot(a, b):
    return jnp.dot(a, b, preferred_element_type=F32)


def _dot_nt(a, b):
    return lax.dot_general(a, b, (((1,), (1,)), ((), ())), preferred_element_type=F32)


def _dot_f32_lhs(x, rhs01):
    return sum(_dot(p, rhs01) for p in _split3(x))


def _dot_f32_rhs(lhs01, x):
    return sum(_dot(lhs01, p) for p in _split3(x))


def _inproj_body(h_ref, g_ref, w_ref, *out_refs):
    hn = _rms(h_ref[...], g_ref[...]).astype(BF16)
    off = 0
    for ref in out_refs:
        width = ref.shape[-1]
        ref[...] = _dot(hn, w_ref[:, off:off + width]).astype(ref.dtype)
        off += width


def _inproj(h, g, w, tm):
    rows = h.shape[0]
    return pl.pallas_call(
        _inproj_body,
        grid=(rows // tm,),
        in_specs=[pl.BlockSpec((tm, D_MODEL), lambda i: (i, 0)),
                  _const_spec((1, D_MODEL)),
                  _const_spec((D_MODEL, IN_WIDTH))],
        out_specs=[pl.BlockSpec((tm, wd), lambda i: (i, 0)) for _, wd, _ in IN_SEGS],
        out_shape=[jax.ShapeDtypeStruct((rows, wd), dt) for _, wd, dt in IN_SEGS],
        compiler_params=_cparams("parallel"),
        name="inproj",
    )(h, g, w)


def _sb_body(q_ref, k_ref, v_ref, o_ref):
    T = BLOCK
    NPAIR = SB_WIDTH // LANES
    NH = SB_HEADS
    i = pl.program_id(1)
    first = lax.broadcasted_iota(jnp.int32, (T, LANES), 1) < SB_HEAD_DIM
    q_rows = []
    for p in range(NPAIR):
        qp = q_ref[:, p * LANES:(p + 1) * LANES]
        qz = jnp.zeros_like(qp)
        q_rows.append(jnp.concatenate([jnp.where(first, qp, qz), jnp.where(first, qz, qp)], axis=0))
    ur = lax.broadcasted_iota(jnp.int32, (T, 2 * T), 0)
    uc = lax.broadcasted_iota(jnp.int32, (T, 2 * T), 1)
    u = jnp.where(ur >= uc, 1.0, jnp.where(uc >= T, 1.0, 0.0)).astype(BF16)

    def tile(j, c, accs, masked):
        start = pl.multiple_of(j * T, T)
        z = jnp.concatenate([_dot_nt(q_rows[p], k_ref[pl.ds(start, T), p * LANES:(p + 1) * LANES])
                             for p in range(NPAIR)], axis=0)
        l = -_softplus(z)
        if masked:
            qpos = i * T + (lax.broadcasted_iota(jnp.int32, (NH * T, T), 0) & (T - 1))
            kpos = start + lax.broadcasted_iota(jnp.int32, (NH * T, T), 1)
            keep = lambda x: jnp.where(kpos < qpos, jnp.where(kpos >= N_PAD, x, 0.0), 0.0)
            l = keep(l)
        l_hi = l.astype(BF16)
        l_lo = (l - l_hi.astype(F32)).astype(BF16)
        cs = _dot(l_hi, u) + _dot(l_lo, u)
        w = jnp.exp(z + c + cs[:, :T])
        if masked:
            w = keep(w)
        w = w.astype(BF16)
        new_accs = []
        for p in range(NPAIR):
            vp = v_ref[pl.ds(start, T), p * LANES:(p + 1) * LANES]
            vz = jnp.zeros_like(vp)
            v_stack = jnp.concatenate([jnp.where(first, vp, vz), jnp.where(first, vz, vp)], axis=0)
            w_pair = jnp.concatenate([w[2 * p * T:(2 * p + 1) * T], w[(2 * p + 1) * T:(2 * p + 2) * T]], axis=1)
            new_accs.append(accs[p] + _dot(w_pair, v_stack))
        c = c + cs[:, T:]
        go = (jnp.max(c) > SB_EXIT).astype(jnp.int32)
        return go, c, new_accs

    def stepper(masked):
        def body(state):
            go, c, accs = tile(state[0], state[2], list(state[3:]), masked)
            return (state[0] - 1, go, c, *accs)
        return body

    c0 = jnp.zeros((NH * T, T), F32)
    acc0 = [jnp.zeros((T, LANES), F32)] * NPAIR
    go, c, accs = tile(i, c0, acc0, True)
    state = (i - 1, go, c, *accs)
    state = lax.while_loop(lambda s: jnp.logical_and(s[0] >= 1, s[1] > 0), stepper(False), state)
    state = lax.while_loop(lambda s: jnp.logical_and(s[0] == 0, s[1] > 0), stepper(True), state)
    for p in range(NPAIR):
        o_ref[:, p * LANES:(p + 1) * LANES] = state[3 + p].astype(o_ref.dtype)


def _sb_attention(q, k, v):
    b, L, _ = q.shape
    T = BLOCK
    return pl.pallas_call(
        _sb_body,
        grid=(b, L // T),
        in_specs=[pl.BlockSpec((None, T, SB_WIDTH), lambda bi, i: (bi, i, 0)),
                  pl.BlockSpec((None, L, SB_WIDTH), lambda bi, i: (bi, 0, 0)),
                  pl.BlockSpec((None, L, SB_WIDTH), lambda bi, i: (bi, 0, 0))],
        out_specs=pl.BlockSpec((None, T, SB_WIDTH), lambda bi, i: (bi, i, 0)),
        out_shape=jax.ShapeDtypeStruct((b, L, SB_WIDTH), BF16),
        compiler_params=_cparams("parallel", "parallel"),
        name="sb_attention",
    )(q, k, v)


def _ssd_body(xbc_ref, z_ref, mb_ref, dtt_ref, cw_ref, cb_ref, dtb_ref, dtbc_ref, af_ref, ac_ref,
              df_ref, ng_ref, o_ref, hist_ref, state_ref):
    c = pl.program_id(1)
    T = BLOCK
    GW = SSM_WIDTH // SSM_GROUPS

    @pl.when(c == 0)
    def _():
        hist_ref[0:HIST, :] = jnp.zeros((HIST, CONV_CH), F32)
        state_ref[...] = jnp.zeros_like(state_ref)

    row = c * T + lax.broadcasted_iota(jnp.int32, (T, 1), 0)
    valid_r = row >= N_PAD
    hist_ref[HIST:HIST + T, :] = jnp.where(valid_r, xbc_ref[...], 0.0)
    conv = cb_ref[...]
    for kk in range(CONV_K):
        off = HIST - (CONV_K - 1) + kk
        conv = conv + cw_ref[kk:kk + 1, :] * hist_ref[off:off + T, :]
    tail = hist_ref[T:T + HIST, :]
    hist_ref[0:HIST, :] = tail
    xc = _silu(conv)
    xs = xc[:, :SSM_WIDTH]
    bm = xc[:, SSM_WIDTH:SSM_WIDTH + SSM_GROUPS * SSM_STATE]
    cm = xc[:, SSM_WIDTH + SSM_GROUPS * SSM_STATE:]

    er = lax.broadcasted_iota(jnp.int32, (LANES, SSM_WIDTH), 0)
    ec = lax.broadcasted_iota(jnp.int32, (LANES, SSM_WIDTH), 1)
    expand = jnp.where(er - DT_LANE0 == ec // SSM_HEAD_DIM, 1.0, 0.0).astype(BF16)
    dt_c = jnp.where(valid_r, _softplus(mb_ref[...] + dtb_ref[...]), 0.0)
    tr = lax.broadcasted_iota(jnp.int32, (T, T), 0)
    tc = lax.broadcasted_iota(jnp.int32, (T, T), 1)
    lower = tr >= tc
    tri = jnp.where(lower, 1.0, 0.0).astype(BF16)
    acs = _dot_f32_rhs(tri, dt_c * af_ref[...])
    acs_last = acs[T - 1:T, :]
    x_dt = xs * _dot_f32_lhs(dt_c, expand)
    x_dt_bf = x_dt.astype(BF16)
    xd_bf = (x_dt * _dot_f32_lhs(jnp.exp(acs_last - acs), expand)).astype(BF16)
    exp_acs = _dot_f32_lhs(jnp.exp(acs), expand)
    chunk_decay = exp_acs[T - 1:T, :]

    colt = c * T + lax.broadcasted_iota(jnp.int32, (1, T), 1)
    dt_t = jnp.where(colt >= N_PAD, _softplus(dtt_ref[...] + dtbc_ref[...]), 0.0)
    tri_t = jnp.where(tr <= tc, 1.0, 0.0).astype(BF16)
    acs_t = _dot_f32_lhs(dt_t * ac_ref[...], tri_t)

    lane = lax.broadcasted_iota(jnp.int32, (T, LANES), 1)
    first = lane < SSM_HEAD_DIM
    y_parts = []
    for g in range(SSM_GROUPS):
        gs = slice(g * GW, (g + 1) * GW)
        bg = bm[:, g * SSM_STATE:(g + 1) * SSM_STATE]
        cg = cm[:, g * SSM_STATE:(g + 1) * SSM_STATE].astype(BF16)
        cb = _dot_nt(cg, bg.astype(BF16))
        prev = state_ref[:, gs]
        y_off = _dot(cg, prev.astype(BF16)) * exp_acs[:, gs]
        state_ref[:, gs] = prev * chunk_decay[:, gs] + _dot(bg.T.astype(BF16), xd_bf[:, gs])
        for pr in range(GW // LANES):
            h0 = (g * GW + pr * LANES) // SSM_HEAD_DIM
            ps = slice(g * GW + pr * LANES, g * GW + (pr + 1) * LANES)
            xp = x_dt_bf[:, ps]
            yd = []
            for h in (h0, h0 + 1):
                seg = acs[:, DT_LANE0 + h:DT_LANE0 + h + 1] - acs_t[h:h + 1, :]
                m = cb * jnp.exp(jnp.where(lower, seg, -jnp.inf))
                yd.append(_dot(m.astype(BF16), xp))
            y_diag = jnp.where(first, yd[0], yd[1])
            y_parts.append(y_diag + y_off[:, pr * LANES:(pr + 1) * LANES] + xs[:, ps] * df_ref[:, ps])
    y = jnp.concatenate(y_parts, axis=-1)
    o_ref[...] = _rms(y * _silu(z_ref[...]), ng_ref[...]).astype(o_ref.dtype)


def _ssd(xbc, z, mb, dt_t, cw, cb, dtb_lanes, dtb_col, a_lanes, a_col, d_full, ng):
    b, L, _ = xbc.shape
    T = BLOCK
    return pl.pallas_call(
        _ssd_body,
        grid=(b, L // T),
        in_specs=[pl.BlockSpec((None, T, CONV_CH), lambda bi, c: (bi, c, 0)),
                  pl.BlockSpec((None, T, SSM_WIDTH), lambda bi, c: (bi, c, 0)),
                  pl.BlockSpec((None, T, LANES), lambda bi, c: (bi, c, 0)),
                  pl.BlockSpec((None, SSM_HEADS, T), lambda bi, c: (bi, 0, c)),
                  _const_spec((CONV_K, CONV_CH)), _const_spec((1, CONV_CH)),
                  _const_spec((1, LANES)), _const_spec((SSM_HEADS, 1)),
                  _const_spec((1, LANES)), _const_spec((SSM_HEADS, 1)),
                  _const_spec((1, SSM_WIDTH)), _const_spec((1, SSM_WIDTH))],
        out_specs=pl.BlockSpec((None, T, SSM_WIDTH), lambda bi, c: (bi, c, 0)),
        out_shape=jax.ShapeDtypeStruct((b, L, SSM_WIDTH), BF16),
        scratch_shapes=[pltpu.VMEM((HIST + T, CONV_CH), F32),
                        pltpu.VMEM((SSM_STATE, SSM_WIDTH), F32)],
        compiler_params=_cparams("parallel", "arbitrary"),
        name="ssd",
    )(xbc, z, mb, dt_t, cw, cb, dtb_lanes, dtb_col, a_lanes, a_col, d_full, ng)


def _mla_prep_body(cq_ref, ckv_ref, ma_ref, mb_ref, gq_ref, gkv_ref, wq_ref, wqr_ref, wk_ref, wv_ref,
                   cosq_ref, sinq_ref, cosk_ref, sink_ref, qx_ref, kx_ref, v_ref):
    scale = math.log2(math.e) / math.sqrt(QK_NOPE + QK_ROPE)
    qn = _rms(cq_ref[...], gq_ref[...]).astype(BF16)
    q = _dot(qn, wq_ref[...])
    qr = _dot(qn, wqr_ref[...])
    cosq = cosq_ref[...]
    sinq = sinq_ref[...]
    kvn = _rms(ckv_ref[...], gkv_ref[...]).astype(BF16)
    kn = _dot(kvn, wk_ref[...])
    v_ref[...] = _dot(kvn, wv_ref[...]).astype(v_ref.dtype)
    k_rope = ma_ref[...] * cosk_ref[...] + mb_ref[...] * sink_ref[...]
    for h in range(MLA_HEADS):
        hs = slice(h * LANES, (h + 1) * LANES)
        qx_ref[:, hs] = ((q[:, hs] * cosq + qr[:, hs] * sinq) * scale).astype(qx_ref.dtype)
        kx_ref[:, hs] = (kn[:, hs] + k_rope).astype(kx_ref.dtype)


def _mla_prep(cq, ckv, ma, mb, gq, gkv, wq, wqr, wk, wv, tabs, tm):
    b, L, _ = cq.shape
    HW = MLA_HEADS * LANES
    row = lambda wd: pl.BlockSpec((None, tm, wd), lambda bi, i: (bi, i, 0))
    tab = pl.BlockSpec((tm, LANES), lambda bi, i: (i, 0))
    return pl.pallas_call(
        _mla_prep_body,
        grid=(b, L // tm),
        in_specs=[row(Q_LORA), row(KV_LORA), row(LANES), row(LANES),
                  _const_spec((1, Q_LORA)), _const_spec((1, KV_LORA)),
                  _const_spec((Q_LORA, HW)), _const_spec((Q_LORA, HW)),
                  _const_spec((KV_LORA, HW)), _const_spec((KV_LORA, MLA_WIDTH)),
                  tab, tab, tab, tab],
        out_specs=[row(HW), row(HW), row(MLA_WIDTH)],
        out_shape=[jax.ShapeDtypeStruct((b, L, HW), BF16), jax.ShapeDtypeStruct((b, L, HW), BF16),
                   jax.ShapeDtypeStruct((b, L, MLA_WIDTH), BF16)],
        compiler_params=_cparams("parallel", "parallel"),
        name="mla_prep",
    )(cq, ckv, ma, mb, gq, gkv, wq, wqr, wk, wv, *tabs)


def _mla_body(q_ref, k_ref, v_ref, o_ref, s_a, s_b, p_a, p_b, al_a, al_b, m_ref, l_ref, acc_ref, *, tq, n_blocks):
    tk = tq
    i = pl.program_id(2)
    n_lane_tiles = tk // LANES
    s_bufs, p_bufs, al_bufs = (s_a, s_b), (p_a, p_b), (al_a, al_b)
    first_v = lax.broadcasted_iota(jnp.int32, (tk, LANES), 1) < V_HEAD
    rc = min(MLA_ROWS, tq)
    first_c = lax.broadcasted_iota(jnp.int32, (rc, LANES), 1) < V_HEAD
    row_c = lax.broadcasted_iota(jnp.int32, (rc, tk), 0)
    col_c = lax.broadcasted_iota(jnp.int32, (rc, tk), 1)

    m_ref[...] = jnp.full(m_ref.shape, NEG_BIG, F32)
    l_ref[...] = jnp.zeros(l_ref.shape, F32)
    acc_ref[...] = jnp.zeros(acc_ref.shape, F32)

    def scores(t, slot):
        start = pl.multiple_of(t * tk, tk)
        for h in range(2):
            s_bufs[slot][h] = _dot_nt(q_ref[:, h * LANES:(h + 1) * LANES],
                                      k_ref[pl.ds(start, tk), h * LANES:(h + 1) * LANES])

    def softmax(slot, mask):
        m_old_all = [m_ref[0], m_ref[1]]
        l_old_all = [l_ref[0], l_ref[1]]
        m_out, l_out, al_out = ([], []), ([], []), []
        for r0 in range(0, tq, rc):
            rows = slice(r0, r0 + rc)
            alphas = []
            for h in range(2):
                s = s_bufs[slot][h, rows, :]
                if mask == "pad":
                    s = jnp.where(col_c <= i * tq + r0 + row_c, jnp.where(col_c >= N_PAD, s, NEG_BIG), NEG_BIG)
                elif mask == "diag":
                    s = jnp.where(col_c <= r0 + row_c, s, NEG_BIG)
                m_old = m_old_all[h][rows, :]
                m_new = jnp.maximum(m_old, jnp.max(s, axis=-1, keepdims=True))
                alpha = jnp.exp2(m_old - m_new)
                p = jnp.exp2(s - jnp.concatenate([m_new] * n_lane_tiles, axis=1))
                m_out[h].append(m_new)
                l_out[h].append(alpha * l_old_all[h][rows, :] + jnp.sum(p, axis=-1, keepdims=True))
                p_bufs[slot][rows, h * tk:(h + 1) * tk] = p.astype(BF16)
                alphas.append(alpha)
            al_out.append(jnp.where(first_c, alphas[0], alphas[1]))
        for h in range(2):
            m_ref[h] = jnp.concatenate(m_out[h], axis=0)
            l_ref[h] = jnp.concatenate(l_out[h], axis=0)
        al_bufs[slot][...] = jnp.concatenate(al_out, axis=0)

    def update(t, slot):
        start = pl.multiple_of(t * tk, tk)
        vt = v_ref[pl.ds(start, tk), :]
        vz = jnp.zeros_like(vt)
        v_stack = jnp.concatenate([jnp.where(first_v, vt, vz), jnp.where(first_v, vz, vt)], axis=0)
        acc_ref[...] = al_bufs[slot][...] * acc_ref[...] + _dot(p_bufs[slot][...], v_stack)

    scores(0, 0)
    if n_blocks > 1:
        scores(1, 1)
    softmax(0, "pad")

    @pl.when(i == 0)
    def _():
        update(0, 0)

    @pl.when(i > 0)
    def _():
        n_plain = i - 1

        def two_tiles(u, _):
            t = 1 + 2 * u
            scores(t + 1, 0)
            update(t - 1, 0)
            softmax(1, None)
            scores(t + 2, 1)
            update(t, 1)
            softmax(0, None)
            return 0

        lax.fori_loop(0, n_plain // 2, two_tiles, 0)

        @pl.when(n_plain % 2 == 1)
        def _():
            scores(i, 0)
            update(i - 2, 0)
            softmax(1, None)

        for parity in (0, 1):
            @pl.when((i & 1) == parity)
            def _(parity=parity):
                update(i - 1, 1 - parity)
                softmax(parity, "diag")
                update(i, parity)

    first_o = lax.broadcasted_iota(jnp.int32, (tq, LANES), 1) < V_HEAD
    o_ref[...] = (acc_ref[...] / jnp.where(first_o, l_ref[0], l_ref[1])).astype(o_ref.dtype)


def _mla_attention(qx, kx, v, tq):
    b, L, _ = qx.shape
    npairs = MLA_WIDTH // LANES
    return pl.pallas_call(
        functools.partial(_mla_body, tq=tq, n_blocks=L // tq),
        grid=(b, npairs, L // tq),
        in_specs=[pl.BlockSpec((None, tq, 2 * LANES), lambda bi, p, i: (bi, i, p)),
                  pl.BlockSpec((None, L, 2 * LANES), lambda bi, p, i: (bi, 0, p)),
                  pl.BlockSpec((None, L, LANES), lambda bi, p, i: (bi, 0, p))],
        out_specs=pl.BlockSpec((None, tq, LANES), lambda bi, p, i: (bi, i, p)),
        out_shape=jax.ShapeDtypeStruct((b, L, MLA_WIDTH), BF16),
        scratch_shapes=[pltpu.VMEM((2, tq, tq), F32), pltpu.VMEM((2, tq, tq), F32),
                        pltpu.VMEM((tq, 2 * tq), BF16), pltpu.VMEM((tq, 2 * tq), BF16),
                        pltpu.VMEM((tq, LANES), F32), pltpu.VMEM((tq, LANES), F32),
                        pltpu.VMEM((2, tq, LANES), F32), pltpu.VMEM((2, tq, LANES), F32),
                        pltpu.VMEM((tq, LANES), F32)],
        compiler_params=_cparams("parallel", "parallel", "parallel"),
        name="mla_attention",
    )(qx, kx, v)


def _mix_ffn_body(h_ref, osb_ref, ossm_ref, omla_ref, wo_ref, g_ref, wg_ref, wu_ref, wd_ref, o_ref, *, ff_chunk):
    h = h_ref[...]
    h = h + _dot(osb_ref[...], wo_ref[0:SB_WIDTH, :])
    h = h + _dot(ossm_ref[...], wo_ref[SB_WIDTH:SB_WIDTH + SSM_WIDTH, :])
    h = h + _dot(omla_ref[...], wo_ref[SB_WIDTH + SSM_WIDTH:, :])
    fn = _rms(h, g_ref[...]).astype(BF16)
    o_ref[...] = h
    for f in range(0, D_FF, ff_chunk):
        gate = _dot(fn, wg_ref[:, f:f + ff_chunk])
        up = _dot(fn, wu_ref[:, f:f + ff_chunk])
        o_ref[...] += _dot((_silu(gate) * up).astype(BF16), wd_ref[f:f + ff_chunk, :])


def _mix_ffn(h, osb, ossm, omla, wo, g, wg, wu, wd, tm, ff_chunk):
    rows = h.shape[0]
    row = lambda wd_: pl.BlockSpec((tm, wd_), lambda i: (i, 0))
    return pl.pallas_call(
        functools.partial(_mix_ffn_body, ff_chunk=ff_chunk),
        grid=(rows // tm,),
        in_specs=[row(D_MODEL), row(SB_WIDTH), row(SSM_WIDTH), row(MLA_WIDTH),
                  _const_spec((MIX_WIDTH, D_MODEL)), _const_spec((1, D_MODEL)),
                  _const_spec((D_MODEL, D_FF)), _const_spec((D_MODEL, D_FF)),
                  _const_spec((D_FF, D_MODEL))],
        out_specs=row(D_MODEL),
        out_shape=jax.ShapeDtypeStruct((rows, D_MODEL), F32),
        compiler_params=_cparams("parallel"),
        name="mix_ffn",
    )(h, osb, ossm, omla, wo, g, wg, wu, wd)


def _final_norm_body(h_ref, g_ref, o_ref):
    o_ref[...] = _rms(h_ref[...], g_ref[...])


def _final_norm(h, g, seq, tm):
    b, L, _ = h.shape
    skip = (L - seq) // tm
    return pl.pallas_call(
        _final_norm_body,
        grid=(b, seq // tm),
        in_specs=[pl.BlockSpec((None, tm, D_MODEL), lambda bi, i: (bi, i + skip, 0)),
                  _const_spec((1, D_MODEL))],
        out_specs=pl.BlockSpec((None, tm, D_MODEL), lambda bi, i: (bi, i, 0)),
        out_shape=jax.ShapeDtypeStruct((b, seq, D_MODEL), F32),
        compiler_params=_cparams("parallel", "parallel"),
        name="final_norm",
    )(h, g)


def _rot_half_cols(w):
    half = QK_ROPE // 2
    return jnp.concatenate([-w[..., half:], w[..., :half]], axis=-1)


def _lane_block(cols, lane0):
    rows, n = cols.shape
    return jnp.pad(cols, ((0, 0), (lane0, LANES - lane0 - n)))


def _prep_in_weight(w_in):
    o = 0
    parts = {}
    for name, wd in (("q", SB_WIDTH), ("k", SB_WIDTH), ("v", SB_WIDTH), ("z", SSM_WIDTH), ("xbc", CONV_CH),
                     ("dt", SSM_HEADS), ("cq", Q_LORA), ("ckv", KV_LORA), ("kr", QK_ROPE)):
        parts[name] = w_in[:, o:o + wd]
        o += wd
    sb_scale = 1.0 / math.sqrt(SB_HEAD_DIM)
    ma = _lane_block(parts["kr"], ROPE_LANE0)
    mb = _lane_block(jnp.concatenate([_rot_half_cols(parts["kr"]), parts["dt"]], axis=-1), ROPE_LANE0)
    return jnp.concatenate([parts["q"] * sb_scale, parts["k"], parts["v"], parts["z"], parts["xbc"],
                            parts["cq"], parts["ckv"], ma, mb], axis=-1).astype(BF16)


def _prep_mla_weights(w_uq, w_ukv):
    dq = QK_NOPE + QK_ROPE
    wq = w_uq.reshape(Q_LORA, MLA_HEADS, dq)
    pad = jnp.zeros((Q_LORA, MLA_HEADS, LANES - dq), F32)
    wq_x = jnp.concatenate([wq, pad], axis=-1)
    wq_r = jnp.concatenate([jnp.zeros((Q_LORA, MLA_HEADS, QK_NOPE), F32), _rot_half_cols(wq[..., QK_NOPE:]), pad],
                           axis=-1)
    wkv = w_ukv.reshape(KV_LORA, MLA_HEADS, QK_NOPE + V_HEAD)
    wk_x = jnp.concatenate([wkv[..., :QK_NOPE], jnp.zeros((KV_LORA, MLA_HEADS, LANES - QK_NOPE), F32)], axis=-1)
    wv = wkv[..., QK_NOPE:]
    flat = lambda a: a.reshape(a.shape[0], -1).astype(BF16)
    return flat(wq_x), flat(wq_r), flat(wk_x), flat(wv)


def _rope_tables(L):
    pos = (jnp.arange(L) - N_PAD).astype(F32)
    inv = ROPE_BASE ** (-jnp.arange(0, QK_ROPE, 2, dtype=F32) / QK_ROPE)
    ang = pos[:, None] * inv[None, :]
    cos2 = jnp.concatenate([jnp.cos(ang), jnp.cos(ang)], axis=-1)
    sin2 = jnp.concatenate([jnp.sin(ang), jnp.sin(ang)], axis=-1)
    cosk = _lane_block(cos2, ROPE_LANE0)
    sink = _lane_block(sin2, ROPE_LANE0)
    cosq = cosk.at[:, :QK_NOPE].set(1.0)
    return cosq, sink, cosk, sink


def _per_head_lanes(p):
    return jnp.repeat(p, SSM_HEAD_DIM)[None, :]


def kernel(x, meta_tokens, mix_norm_g, w_in, conv_w, conv_b, dt_bias, a_log, d_skip, ssm_norm_g, q_norm_g, w_uq,
           kv_norm_g, w_ukv, w_out, ffn_norm_g, w_gate, w_up, w_down, final_norm_g):
    b, seq, _ = x.shape
    depth = w_in.shape[0]
    L = seq + BLOCK
    rows = b * L
    tm = 640 if rows % 640 == 0 else BLOCK
    tm_b = 640 if L % 640 == 0 else BLOCK
    h = jnp.concatenate([jnp.zeros((b, N_PAD, D_MODEL), F32),
                         jnp.broadcast_to(meta_tokens[None], (b, N_META, D_MODEL)), x], axis=1).reshape(rows, D_MODEL)
    tabs = _rope_tables(L)
    r3 = lambda a: a.reshape(b, L, a.shape[-1])
    for i in range(depth):
        w_in_p = _prep_in_weight(w_in[i])
        q, k, v, z, xbc, cq, ckv, ma, mb = _inproj(h, mix_norm_g[i][None, :], w_in_p, tm)
        o_sb = _sb_attention(r3(q), r3(k), r3(v))
        a = -jnp.exp(a_log[i])
        mb3 = r3(mb)
        dt_t = jnp.swapaxes(mb3[:, :, DT_LANE0:DT_LANE0 + SSM_HEADS], 1, 2)
        o_ssm = _ssd(r3(xbc), r3(z), mb3, dt_t, conv_w[i], conv_b[i][None, :],
                     _lane_block(dt_bias[i][None, :], DT_LANE0), dt_bias[i][:, None],
                     _lane_block(a[None, :], DT_LANE0), a[:, None],
                     _per_head_lanes(d_skip[i]), ssm_norm_g[i][None, :])
        wq_x, wq_r, wk_x, wv = _prep_mla_weights(w_uq[i], w_ukv[i])
        qx, kx, vm = _mla_prep(r3(cq), r3(ckv), r3(ma), mb3, q_norm_g[i][None, :], kv_norm_g[i][None, :],
                               wq_x, wq_r, wk_x, wv, tabs, tm_b)
        o_mla = _mla_attention(qx, kx, vm, tm_b)
        h = _mix_ffn(h, o_sb.reshape(rows, SB_WIDTH), o_ssm.reshape(rows, SSM_WIDTH), o_mla.reshape(rows, MLA_WIDTH),
                     w_out[i].astype(BF16), ffn_norm_g[i][None, :], w_gate[i].astype(BF16), w_up[i].astype(BF16),
                     w_down[i].astype(BF16), tm, 704)
    return _final_norm(h.reshape(b, L, D_MODEL), final_norm_g[None, :], seq, BLOCK)
```

```python
import functools
import math

import jax
import jax.numpy as jnp
from jax import lax
from jax.experimental import pallas as pl
from jax.experimental.pallas import tpu as pltpu

F32 = jnp.float32
BF16 = jnp.bfloat16

D_MODEL = 1024
N_META = 16
BLOCK = 128
N_PAD = BLOCK - N_META
SB_HEADS = 8
SB_HEAD_DIM = 64
SB_WIDTH = SB_HEADS * SB_HEAD_DIM
SSM_HEADS = 16
SSM_HEAD_DIM = 64
SSM_WIDTH = SSM_HEADS * SSM_HEAD_DIM
SSM_GROUPS = 2
SSM_STATE = 128
CONV_K = 4
CONV_CH = SSM_WIDTH + 2 * SSM_GROUPS * SSM_STATE
MLA_HEADS = 8
Q_LORA = 384
KV_LORA = 256
QK_NOPE = 64
QK_ROPE = 32
V_HEAD = 64
MLA_WIDTH = MLA_HEADS * V_HEAD
ROPE_BASE = 10000.0
MIX_WIDTH = SB_WIDTH + SSM_WIDTH + MLA_WIDTH
D_FF = 2816
EPS = 1e-6

LANES = 128
ROPE_LANE0 = QK_NOPE
DT_LANE0 = QK_NOPE + QK_ROPE
MLA_MASK_LANE = QK_NOPE + QK_ROPE
HIST = 8
SB_EXIT = -160.0
NEG_BIG = -1e30
MLA_ROWS = 160
VMEM_LIMIT = 56 * 1024 * 1024

IN_SEGS = (("q", SB_WIDTH, BF16), ("k", SB_WIDTH, BF16), ("v", SB_WIDTH, BF16),
           ("z", SSM_WIDTH, F32), ("xbc", CONV_CH, F32), ("cq", Q_LORA, F32),
           ("ckv", KV_LORA, F32), ("ma", LANES, F32), ("mb", LANES, F32))
IN_WIDTH = sum(s[1] for s in IN_SEGS)


def _cparams(*sem):
    return pltpu.CompilerParams(dimension_semantics=sem, vmem_limit_bytes=VMEM_LIMIT)


def _const_spec(shape):
    nd = len(shape)
    return pl.BlockSpec(shape, lambda *_: (0,) * nd, pipeline_mode=pl.Buffered(1))


def _layer_spec(shape, layer):
    nd = len(shape)
    return pl.BlockSpec((None,) + tuple(shape), lambda *_: (layer,) + (0,) * nd, pipeline_mode=pl.Buffered(1))


def _rms(x, g):
    return x * lax.rsqrt(jnp.mean(x * x, axis=-1, keepdims=True) + EPS) * g


def _silu(x):
    return x * (1.0 / (1.0 + jnp.exp(-x)))


def _softplus(x):
    return jnp.maximum(x, 0.0) + jnp.log(1.0 + jnp.exp(-jnp.abs(x)))


def _split3(x):
    hi = x.astype(BF16)
    r1 = x - hi.astype(F32)
    mid = r1.astype(BF16)
    lo = (r1 - mid.astype(F32)).astype(BF16)
    return hi, mid, lo


def _dot(a, b):
    return jnp.dot(a, b, preferred_element_type=F32)


def _dot_nt(a, b):
    return lax.dot_general(a, b, (((1,), (1,)), ((), ())), preferred_element_type=F32)


def _dot_f32_lhs(x, rhs01):
    return sum(_dot(p, rhs01) for p in _split3(x))


def _dot_f32_rhs(lhs01, x):
    return sum(_dot(lhs01, p) for p in _split3(x))


def _inproj_body(h_ref, g_ref, w_ref, *out_refs):
    hn = _rms(h_ref[...], g_ref[...]).astype(BF16)
    off = 0
    for ref in out_refs:
        width = ref.shape[-1]
        ref[...] = _dot(hn, w_ref[:, off:off + width]).astype(ref.dtype)
        off += width


def _inproj(h, g, w, layer, tm):
    rows = h.shape[0]
    return pl.pallas_call(
        _inproj_body,
        grid=(rows // tm,),
        in_specs=[pl.BlockSpec((tm, D_MODEL), lambda i: (i, 0)),
                  _const_spec((1, D_MODEL)),
                  _layer_spec((D_MODEL, IN_WIDTH), layer)],
        out_specs=[pl.BlockSpec((tm, wd), lambda i: (i, 0)) for _, wd, _ in IN_SEGS],
        out_shape=[jax.ShapeDtypeStruct((rows, wd), dt) for _, wd, dt in IN_SEGS],
        compiler_params=_cparams("parallel"),
        name="inproj",
    )(h, g, w)


def _sb_body(q_ref, k_ref, v_ref, o_ref):
    T = BLOCK
    NPAIR = SB_WIDTH // LANES
    NH = SB_HEADS
    i = pl.program_id(1)
    first = lax.broadcasted_iota(jnp.int32, (T, LANES), 1) < SB_HEAD_DIM
    q_rows = []
    for p in range(NPAIR):
        qp = q_ref[:, p * LANES:(p + 1) * LANES]
        qz = jnp.zeros_like(qp)
        q_rows.append(jnp.concatenate([jnp.where(first, qp, qz), jnp.where(first, qz, qp)], axis=0))
    ur = lax.broadcasted_iota(jnp.int32, (T, 2 * T), 0)
    uc = lax.broadcasted_iota(jnp.int32, (T, 2 * T), 1)
    u = jnp.where(ur >= uc, -1.0, jnp.where(uc >= T, -1.0, 0.0)).astype(BF16)
    u2 = jnp.concatenate([u, u], axis=0)

    def tile(j, c, accs, masked):
        start = pl.multiple_of(j * T, T)
        z = jnp.concatenate([_dot_nt(q_rows[p], k_ref[pl.ds(start, T), p * LANES:(p + 1) * LANES])
                             for p in range(NPAIR)], axis=0)
        nl = jnp.maximum(z, 0.0) + jnp.log2(1.0 + jnp.exp2(-jnp.abs(z)))
        if masked:
            qpos = i * T + (lax.broadcasted_iota(jnp.int32, (NH * T, T), 0) & (T - 1))
            kpos = start + lax.broadcasted_iota(jnp.int32, (NH * T, T), 1)
            keep = lambda x: jnp.where(kpos < qpos, jnp.where(kpos >= N_PAD, x, 0.0), 0.0)
            nl = keep(nl)
        nl_hi = nl.astype(BF16)
        nl_lo = (nl - nl_hi.astype(F32)).astype(BF16)
        cs = _dot(jnp.concatenate([nl_hi, nl_lo], axis=1), u2)
        w = jnp.exp2(z + c + cs[:, :T])
        if masked:
            w = keep(w)
        w = w.astype(BF16)
        new_accs = []
        for p in range(NPAIR):
            vp = v_ref[pl.ds(start, T), p * LANES:(p + 1) * LANES]
            vz = jnp.zeros_like(vp)
            v_stack = jnp.concatenate([jnp.where(first, vp, vz), jnp.where(first, vz, vp)], axis=0)
            w_pair = jnp.concatenate([w[2 * p * T:(2 * p + 1) * T], w[(2 * p + 1) * T:(2 * p + 2) * T]], axis=1)
            new_accs.append(accs[p] + _dot(w_pair, v_stack))
        c = c + cs[:, T:]
        go = (jnp.max(c) > SB_EXIT).astype(jnp.int32)
        return go, c, new_accs

    def stepper(masked):
        def body(state):
            go, c, accs = tile(state[0], state[2], list(state[3:]), masked)
            return (state[0] - 1, go, c, *accs)
        return body

    c0 = jnp.zeros((NH * T, T), F32)
    acc0 = [jnp.zeros((T, LANES), F32)] * NPAIR
    def diag_only():
        go, c, accs = tile(i, c0, acc0, True)
        return (i - 1, go, c, *accs)

    def diag_and_two():
        _, c, accs = tile(i, c0, acc0, True)
        _, c, accs = tile(i - 1, c, accs, False)
        go, c, accs = tile(i - 2, c, accs, False)
        return (i - 3, go, c, *accs)

    state = lax.cond(i >= 3, diag_and_two, diag_only)

    def two_tiles(state):
        _, c, accs = tile(state[0], state[2], list(state[3:]), False)
        go, c, accs = tile(state[0] - 1, c, accs, False)
        return (state[0] - 2, go, c, *accs)

    state = lax.while_loop(lambda s: jnp.logical_and(s[0] >= 2, s[1] > 0), two_tiles, state)
    state = lax.while_loop(lambda s: jnp.logical_and(s[0] >= 1, s[1] > 0), stepper(False), state)
    state = lax.while_loop(lambda s: jnp.logical_and(s[0] == 0, s[1] > 0), stepper(True), state)
    for p in range(NPAIR):
        o_ref[:, p * LANES:(p + 1) * LANES] = state[3 + p].astype(o_ref.dtype)


def _sb_attention(q, k, v):
    b, L, _ = q.shape
    T = BLOCK
    return pl.pallas_call(
        _sb_body,
        grid=(b, L // T),
        in_specs=[pl.BlockSpec((None, T, SB_WIDTH), lambda bi, i: (bi, i, 0)),
                  pl.BlockSpec((None, L, SB_WIDTH), lambda bi, i: (bi, 0, 0)),
                  pl.BlockSpec((None, L, SB_WIDTH), lambda bi, i: (bi, 0, 0))],
        out_specs=pl.BlockSpec((None, T, SB_WIDTH), lambda bi, i: (bi, i, 0)),
        out_shape=jax.ShapeDtypeStruct((b, L, SB_WIDTH), BF16),
        compiler_params=_cparams("parallel", "parallel"),
        name="sb_attention",
    )(q, k, v)


def _ssd_body(xbc_ref, z_ref, mb_ref, cw_ref, cb_ref, dtb_ref, af_ref, df_ref, ng_ref, o_ref, hist_ref, state_ref):
    c = pl.program_id(1)
    T = BLOCK
    GW = SSM_WIDTH // SSM_GROUPS

    @pl.when(c == 0)
    def _():
        hist_ref[0:HIST, :] = jnp.zeros((HIST, CONV_CH), F32)
        state_ref[...] = jnp.zeros_like(state_ref)

    row = c * T + lax.broadcasted_iota(jnp.int32, (T, 1), 0)
    valid_r = row >= N_PAD
    hist_ref[HIST:HIST + T, :] = jnp.where(valid_r, xbc_ref[...], 0.0)
    conv = cb_ref[...]
    for kk in range(CONV_K):
        off = HIST - (CONV_K - 1) + kk
        conv = conv + cw_ref[kk:kk + 1, :] * hist_ref[off:off + T, :]
    tail = hist_ref[T:T + HIST, :]
    hist_ref[0:HIST, :] = tail
    xc = _silu(conv)
    xs = xc[:, :SSM_WIDTH]
    bm = xc[:, SSM_WIDTH:SSM_WIDTH + SSM_GROUPS * SSM_STATE]
    cm = xc[:, SSM_WIDTH + SSM_GROUPS * SSM_STATE:]

    er = lax.broadcasted_iota(jnp.int32, (LANES, SSM_WIDTH), 0)
    ec = lax.broadcasted_iota(jnp.int32, (LANES, SSM_WIDTH), 1)
    expand = jnp.where(er - DT_LANE0 == ec // SSM_HEAD_DIM, 1.0, 0.0).astype(BF16)
    dt_c = jnp.where(valid_r, _softplus(mb_ref[...] + dtb_ref[...]), 0.0)
    tr = lax.broadcasted_iota(jnp.int32, (T, T), 0)
    tc = lax.broadcasted_iota(jnp.int32, (T, T), 1)
    lower = tr >= tc
    tri = jnp.where(lower, 1.0, 0.0).astype(BF16)
    acs = _dot_f32_rhs(tri, dt_c * af_ref[...])
    acs_last = acs[T - 1:T, :]
    x_dt = xs * _dot_f32_lhs(dt_c, expand)
    x_dt_bf = x_dt.astype(BF16)
    xd_bf = (x_dt * _dot_f32_lhs(jnp.exp(acs_last - acs), expand)).astype(BF16)
    exp_acs = _dot_f32_lhs(jnp.exp(acs), expand)
    chunk_decay = exp_acs[T - 1:T, :]

    acs_t = acs.T

    lane = lax.broadcasted_iota(jnp.int32, (T, LANES), 1)
    first = lane < SSM_HEAD_DIM
    y_parts = []
    for g in range(SSM_GROUPS):
        gs = slice(g * GW, (g + 1) * GW)
        bg = bm[:, g * SSM_STATE:(g + 1) * SSM_STATE]
        cg = cm[:, g * SSM_STATE:(g + 1) * SSM_STATE].astype(BF16)
        cb = _dot_nt(cg, bg.astype(BF16))
        prev = state_ref[:, gs]
        y_off = _dot(cg, prev.astype(BF16)) * exp_acs[:, gs]
        state_ref[:, gs] = prev * chunk_decay[:, gs] + _dot(bg.T.astype(BF16), xd_bf[:, gs])
        for pr in range(GW // LANES):
            h0 = (g * GW + pr * LANES) // SSM_HEAD_DIM
            ps = slice(g * GW + pr * LANES, g * GW + (pr + 1) * LANES)
            xp = x_dt_bf[:, ps]
            yd = []
            for h in (h0, h0 + 1):
                seg = acs[:, DT_LANE0 + h:DT_LANE0 + h + 1] - acs_t[DT_LANE0 + h:DT_LANE0 + h + 1, :]
                m = cb * jnp.exp(jnp.where(lower, seg, -jnp.inf))
                yd.append(_dot(m.astype(BF16), xp))
            y_diag = jnp.where(first, yd[0], yd[1])
            y_parts.append(y_diag + y_off[:, pr * LANES:(pr + 1) * LANES] + xs[:, ps] * df_ref[:, ps])
    y = jnp.concatenate(y_parts, axis=-1)
    o_ref[...] = _rms(y * _silu(z_ref[...]), ng_ref[...]).astype(o_ref.dtype)


def _ssd(xbc, z, mb, cw, cb, dtb_lanes, a_lanes, d_full, ng):
    b, L, _ = xbc.shape
    T = BLOCK
    return pl.pallas_call(
        _ssd_body,
        grid=(b, L // T),
        in_specs=[pl.BlockSpec((None, T, CONV_CH), lambda bi, c: (bi, c, 0)),
                  pl.BlockSpec((None, T, SSM_WIDTH), lambda bi, c: (bi, c, 0)),
                  pl.BlockSpec((None, T, LANES), lambda bi, c: (bi, c, 0)),
                  _const_spec((CONV_K, CONV_CH)), _const_spec((1, CONV_CH)),
                  _const_spec((1, LANES)), _const_spec((1, LANES)),
                  _const_spec((1, SSM_WIDTH)), _const_spec((1, SSM_WIDTH))],
        out_specs=pl.BlockSpec((None, T, SSM_WIDTH), lambda bi, c: (bi, c, 0)),
        out_shape=jax.ShapeDtypeStruct((b, L, SSM_WIDTH), BF16),
        scratch_shapes=[pltpu.VMEM((HIST + T, CONV_CH), F32),
                        pltpu.VMEM((SSM_STATE, SSM_WIDTH), F32)],
        compiler_params=_cparams("parallel", "arbitrary"),
        name="ssd",
    )(xbc, z, mb, cw, cb, dtb_lanes, a_lanes, d_full, ng)


def _mla_prep_body(cq_ref, ckv_ref, ma_ref, mb_ref, gq_ref, gkv_ref, wq_ref, wqr_ref, wk_ref, wv_ref,
                   cosq_ref, sinq_ref, cosk_ref, sink_ref, qx_ref, kx_ref, v_ref):
    scale = math.log2(math.e) / math.sqrt(QK_NOPE + QK_ROPE)
    qn = _rms(cq_ref[...], gq_ref[...]).astype(BF16)
    q = _dot(qn, wq_ref[...])
    qr = _dot(qn, wqr_ref[...])
    cosq = cosq_ref[...]
    sinq = sinq_ref[...]
    kvn = _rms(ckv_ref[...], gkv_ref[...]).astype(BF16)
    kn = _dot(kvn, wk_ref[...])
    v_ref[...] = _dot(kvn, wv_ref[...]).astype(v_ref.dtype)
    k_rope = ma_ref[...] * cosk_ref[...] + mb_ref[...] * sink_ref[...]
    tm = q.shape[0]
    lane = lax.broadcasted_iota(jnp.int32, (tm, LANES), 1)
    pos = pl.program_id(1) * tm + lax.broadcasted_iota(jnp.int32, (tm, LANES), 0)
    mask_lane = lane == MLA_MASK_LANE
    q_bias = jnp.where(mask_lane, 1.0, 0.0)
    k_bias = k_rope + jnp.where(mask_lane, jnp.where(pos < N_PAD, NEG_BIG, 0.0), 0.0)
    for h in range(MLA_HEADS):
        hs = slice(h * LANES, (h + 1) * LANES)
        qx_ref[:, hs] = ((q[:, hs] * cosq + qr[:, hs] * sinq) * scale + q_bias).astype(qx_ref.dtype)
        kx_ref[:, hs] = (kn[:, hs] + k_bias).astype(kx_ref.dtype)


def _mla_prep(cq, ckv, ma, mb, gq, gkv, wq, wqr, wk, wv, layer, tabs, tm):
    b, L, _ = cq.shape
    HW = MLA_HEADS * LANES
    row = lambda wd: pl.BlockSpec((None, tm, wd), lambda bi, i: (bi, i, 0))
    tab = pl.BlockSpec((tm, LANES), lambda bi, i: (i, 0))
    return pl.pallas_call(
        _mla_prep_body,
        grid=(b, L // tm),
        in_specs=[row(Q_LORA), row(KV_LORA), row(LANES), row(LANES),
                  _const_spec((1, Q_LORA)), _const_spec((1, KV_LORA)),
                  _layer_spec((Q_LORA, HW), layer), _layer_spec((Q_LORA, HW), layer),
                  _layer_spec((KV_LORA, HW), layer), _layer_spec((KV_LORA, MLA_WIDTH), layer),
                  tab, tab, tab, tab],
        out_specs=[row(HW), row(HW), row(MLA_WIDTH)],
        out_shape=[jax.ShapeDtypeStruct((b, L, HW), BF16), jax.ShapeDtypeStruct((b, L, HW), BF16),
                   jax.ShapeDtypeStruct((b, L, MLA_WIDTH), BF16)],
        compiler_params=_cparams("parallel", "parallel"),
        name="mla_prep",
    )(cq, ckv, ma, mb, gq, gkv, wq, wqr, wk, wv, *tabs)


def _mla_body(q_ref, k_ref, v_ref, o_ref, s_a, s_b, p_a, p_b, al_a, al_b, m_ref, l_ref, acc_ref, *, tq, n_blocks):
    tk = tq
    i = pl.program_id(2)
    n_lane_tiles = tk // LANES
    s_bufs, p_bufs, al_bufs = (s_a, s_b), (p_a, p_b), (al_a, al_b)
    first_v = lax.broadcasted_iota(jnp.int32, (tk, LANES), 1) < V_HEAD
    rc = min(MLA_ROWS, tq)
    first_c = lax.broadcasted_iota(jnp.int32, (rc, LANES), 1) < V_HEAD
    row_c = lax.broadcasted_iota(jnp.int32, (rc, tk), 0)
    col_c = lax.broadcasted_iota(jnp.int32, (rc, tk), 1)

    m_ref[...] = jnp.full(m_ref.shape, NEG_BIG, F32)
    l_ref[...] = jnp.zeros(l_ref.shape, F32)
    acc_ref[...] = jnp.zeros(acc_ref.shape, F32)

    def scores(t, slot):
        start = pl.multiple_of(t * tk, tk)
        for h in range(2):
            s_bufs[slot][h] = _dot_nt(q_ref[:, h * LANES:(h + 1) * LANES],
                                      k_ref[pl.ds(start, tk), h * LANES:(h + 1) * LANES])

    def softmax(slot, mask):
        m_old_all = [m_ref[0], m_ref[1]]
        l_old_all = [l_ref[0], l_ref[1]]
        m_out, l_out, al_out = ([], []), ([], []), []
        for r0 in range(0, tq, rc):
            rows = slice(r0, r0 + rc)
            alphas = []
            for h in range(2):
                s = s_bufs[slot][h, rows, :]
                if mask == "first":
                    s = jnp.where(col_c <= i * tq + r0 + row_c, s, NEG_BIG)
                elif mask == "diag":
                    s = jnp.where(col_c <= r0 + row_c, s, NEG_BIG)
                m_old = m_old_all[h][rows, :]
                m_new = jnp.maximum(m_old, jnp.max(s, axis=-1, keepdims=True))
                alpha = jnp.exp2(m_old - m_new)
                p = jnp.exp2(s - jnp.concatenate([m_new] * n_lane_tiles, axis=1))
                m_out[h].append(m_new)
                l_out[h].append(alpha * l_old_all[h][rows, :] + jnp.sum(p, axis=-1, keepdims=True))
                p_bufs[slot][rows, h * tk:(h + 1) * tk] = p.astype(BF16)
                alphas.append(alpha)
            al_out.append(jnp.where(first_c, alphas[0], alphas[1]))
        for h in range(2):
            m_ref[h] = jnp.concatenate(m_out[h], axis=0)
            l_ref[h] = jnp.concatenate(l_out[h], axis=0)
        al_bufs[slot][...] = jnp.concatenate(al_out, axis=0)

    def update(t, slot):
        start = pl.multiple_of(t * tk, tk)
        vt = v_ref[pl.ds(start, tk), :]
        vz = jnp.zeros_like(vt)
        v_stack = jnp.concatenate([jnp.where(first_v, vt, vz), jnp.where(first_v, vz, vt)], axis=0)
        acc_ref[...] = al_bufs[slot][...] * acc_ref[...] + _dot(p_bufs[slot][...], v_stack)

    scores(0, 0)
    if n_blocks > 1:
        scores(1, 1)
    softmax(0, "first")

    @pl.when(i == 0)
    def _():
        update(0, 0)

    @pl.when(i > 0)
    def _():
        n_plain = i - 1

        def two_tiles(u, _):
            t = 1 + 2 * u
            scores(t + 1, 0)
            update(t - 1, 0)
            softmax(1, None)
            scores(t + 2, 1)
            update(t, 1)
            softmax(0, None)
            return 0

        lax.fori_loop(0, n_plain // 2, two_tiles, 0)

        @pl.when(n_plain % 2 == 1)
        def _():
            scores(i, 0)
            update(i - 2, 0)
            softmax(1, None)

        for parity in (0, 1):
            @pl.when((i & 1) == parity)
            def _(parity=parity):
                update(i - 1, 1 - parity)
                softmax(parity, "diag")
                update(i, parity)

    first_o = lax.broadcasted_iota(jnp.int32, (tq, LANES), 1) < V_HEAD
    o_ref[...] = (acc_ref[...] / jnp.where(first_o, l_ref[0], l_ref[1])).astype(o_ref.dtype)


def _mla_attention(qx, kx, v, tq):
    b, L, _ = qx.shape
    npairs = MLA_WIDTH // LANES
    return pl.pallas_call(
        functools.partial(_mla_body, tq=tq, n_blocks=L // tq),
        grid=(b, npairs, L // tq),
        in_specs=[pl.BlockSpec((None, tq, 2 * LANES), lambda bi, p, i: (bi, i, p)),
                  pl.BlockSpec((None, L, 2 * LANES), lambda bi, p, i: (bi, 0, p)),
                  pl.BlockSpec((None, L, LANES), lambda bi, p, i: (bi, 0, p))],
        out_specs=pl.BlockSpec((None, tq, LANES), lambda bi, p, i: (bi, i, p)),
        out_shape=jax.ShapeDtypeStruct((b, L, MLA_WIDTH), BF16),
        scratch_shapes=[pltpu.VMEM((2, tq, tq), F32), pltpu.VMEM((2, tq, tq), F32),
                        pltpu.VMEM((tq, 2 * tq), BF16), pltpu.VMEM((tq, 2 * tq), BF16),
                        pltpu.VMEM((tq, LANES), F32), pltpu.VMEM((tq, LANES), F32),
                        pltpu.VMEM((2, tq, LANES), F32), pltpu.VMEM((2, tq, LANES), F32),
                        pltpu.VMEM((tq, LANES), F32)],
        compiler_params=_cparams("parallel", "parallel", "parallel"),
        name="mla_attention",
    )(qx, kx, v)


def _mix_ffn_body(h_ref, osb_ref, ossm_ref, omla_ref, wo_ref, g_ref, wg_ref, wu_ref, wd_ref, o_ref, *, ff_chunk):
    h = h_ref[...]
    h = h + _dot(osb_ref[...], wo_ref[0:SB_WIDTH, :])
    h = h + _dot(ossm_ref[...], wo_ref[SB_WIDTH:SB_WIDTH + SSM_WIDTH, :])
    h = h + _dot(omla_ref[...], wo_ref[SB_WIDTH + SSM_WIDTH:, :])
    fn = _rms(h, g_ref[...]).astype(BF16)
    o_ref[...] = h
    for f in range(0, D_FF, ff_chunk):
        gate = _dot(fn, wg_ref[:, f:f + ff_chunk])
        up = _dot(fn, wu_ref[:, f:f + ff_chunk])
        o_ref[...] += _dot((_silu(gate) * up).astype(BF16), wd_ref[f:f + ff_chunk, :])


def _mix_ffn(h, osb, ossm, omla, wo, g, wg, wu, wd, layer, tm, ff_chunk):
    rows = h.shape[0]
    row = lambda wd_: pl.BlockSpec((tm, wd_), lambda i: (i, 0))
    return pl.pallas_call(
        functools.partial(_mix_ffn_body, ff_chunk=ff_chunk),
        grid=(rows // tm,),
        in_specs=[row(D_MODEL), row(SB_WIDTH), row(SSM_WIDTH), row(MLA_WIDTH),
                  _layer_spec((MIX_WIDTH, D_MODEL), layer), _const_spec((1, D_MODEL)),
                  _layer_spec((D_MODEL, D_FF), layer), _layer_spec((D_MODEL, D_FF), layer),
                  _layer_spec((D_FF, D_MODEL), layer)],
        out_specs=row(D_MODEL),
        out_shape=jax.ShapeDtypeStruct((rows, D_MODEL), F32),
        compiler_params=_cparams("parallel"),
        name="mix_ffn",
    )(h, osb, ossm, omla, wo, g, wg, wu, wd)


def _final_norm_body(h_ref, g_ref, o_ref):
    o_ref[...] = _rms(h_ref[...], g_ref[...])


def _final_norm(h, g, b, seq, tm):
    L = h.shape[0] // b
    return pl.pallas_call(
        _final_norm_body,
        grid=(b, seq // tm),
        in_specs=[pl.BlockSpec((pl.Element(tm), pl.Element(D_MODEL)),
                               lambda bi, i: (pl.multiple_of(bi * L + i * tm + (L - seq), BLOCK), 0)),
                  _const_spec((1, D_MODEL))],
        out_specs=pl.BlockSpec((None, tm, D_MODEL), lambda bi, i: (bi, i, 0)),
        out_shape=jax.ShapeDtypeStruct((b, seq, D_MODEL), F32),
        compiler_params=_cparams("parallel", "parallel"),
        name="final_norm",
    )(h, g)


def _rot_half_cols(w):
    half = QK_ROPE // 2
    return jnp.concatenate([-w[..., half:], w[..., :half]], axis=-1)


def _lane_block(cols, lane0):
    n = cols.shape[-1]
    return jnp.pad(cols, ((0, 0),) * (cols.ndim - 1) + ((lane0, LANES - lane0 - n),))


def _prep_in_weights(w_in):
    o = 0
    parts = {}
    for name, wd in (("q", SB_WIDTH), ("k", SB_WIDTH), ("v", SB_WIDTH), ("z", SSM_WIDTH), ("xbc", CONV_CH),
                     ("dt", SSM_HEADS), ("cq", Q_LORA), ("ckv", KV_LORA), ("kr", QK_ROPE)):
        parts[name] = w_in[..., o:o + wd]
        o += wd
    sb_scale = math.log2(math.e) / math.sqrt(SB_HEAD_DIM)
    ma = _lane_block(parts["kr"], ROPE_LANE0)
    mb = _lane_block(jnp.concatenate([_rot_half_cols(parts["kr"]), parts["dt"]], axis=-1), ROPE_LANE0)
    return jnp.concatenate([parts["q"] * sb_scale, parts["k"], parts["v"], parts["z"], parts["xbc"],
                            parts["cq"], parts["ckv"], ma, mb], axis=-1).astype(BF16)


def _prep_mla_weights(w_uq, w_ukv):
    depth = w_uq.shape[0]
    dq = QK_NOPE + QK_ROPE
    wq = w_uq.reshape(depth, Q_LORA, MLA_HEADS, dq)
    pad = jnp.zeros((depth, Q_LORA, MLA_HEADS, LANES - dq), F32)
    wq_x = jnp.concatenate([wq, pad], axis=-1)
    wq_r = jnp.concatenate([jnp.zeros((depth, Q_LORA, MLA_HEADS, QK_NOPE), F32),
                            _rot_half_cols(wq[..., QK_NOPE:]), pad], axis=-1)
    wkv = w_ukv.reshape(depth, KV_LORA, MLA_HEADS, QK_NOPE + V_HEAD)
    wk_x = jnp.concatenate([wkv[..., :QK_NOPE], jnp.zeros((depth, KV_LORA, MLA_HEADS, LANES - QK_NOPE), F32)],
                           axis=-1)
    wv = wkv[..., QK_NOPE:]
    flat = lambda a: a.reshape(a.shape[0], a.shape[1], -1).astype(BF16)
    return flat(wq_x), flat(wq_r), flat(wk_x), flat(wv)


def _rope_tables(L):
    pos = (jnp.arange(L) - N_PAD).astype(F32)
    inv = ROPE_BASE ** (-jnp.arange(0, QK_ROPE, 2, dtype=F32) / QK_ROPE)
    ang = pos[:, None] * inv[None, :]
    cos2 = jnp.concatenate([jnp.cos(ang), jnp.cos(ang)], axis=-1)
    sin2 = jnp.concatenate([jnp.sin(ang), jnp.sin(ang)], axis=-1)
    tail = jnp.zeros((L, LANES - QK_NOPE - QK_ROPE), F32)
    head0 = jnp.zeros((L, QK_NOPE), F32)
    cosk = jnp.concatenate([head0, cos2, tail], axis=-1)
    sink = jnp.concatenate([head0, sin2, tail], axis=-1)
    cosq = jnp.concatenate([jnp.ones((L, QK_NOPE), F32), cos2, tail], axis=-1)
    return cosq, sink, cosk, sink


def _per_head_lanes(p):
    return jnp.repeat(p, SSM_HEAD_DIM)[None, :]


def kernel(x, meta_tokens, mix_norm_g, w_in, conv_w, conv_b, dt_bias, a_log, d_skip, ssm_norm_g, q_norm_g, w_uq,
           kv_norm_g, w_ukv, w_out, ffn_norm_g, w_gate, w_up, w_down, final_norm_g):
    b, seq, _ = x.shape
    depth = w_in.shape[0]
    L = seq + BLOCK
    rows = b * L
    tm = 640 if rows % 640 == 0 else BLOCK
    tm_b = 640 if L % 640 == 0 else BLOCK
    h = jnp.concatenate([jnp.zeros((b, N_PAD, D_MODEL), F32),
                         jnp.broadcast_to(meta_tokens[None], (b, N_META, D_MODEL)), x], axis=1).reshape(rows, D_MODEL)
    tabs = _rope_tables(L)
    w_in_p = _prep_in_weights(w_in)
    wq_x, wq_r, wk_x, wv = _prep_mla_weights(w_uq, w_ukv)
    wo_b, wg_b, wu_b, wd_b = (w.astype(BF16) for w in (w_out, w_gate, w_up, w_down))
    r3 = lambda a: a.reshape(b, L, a.shape[-1])
    for i in range(depth):
        q, k, v, z, xbc, cq, ckv, ma, mb = _inproj(h, mix_norm_g[i][None, :], w_in_p, i, tm)
        o_sb = _sb_attention(r3(q), r3(k), r3(v))
        a = -jnp.exp(a_log[i])
        mb3 = r3(mb)
        o_ssm = _ssd(r3(xbc), r3(z), mb3, conv_w[i], conv_b[i][None, :],
                     _lane_block(dt_bias[i][None, :], DT_LANE0), _lane_block(a[None, :], DT_LANE0),
                     _per_head_lanes(d_skip[i]), ssm_norm_g[i][None, :])
        qx, kx, vm = _mla_prep(r3(cq), r3(ckv), r3(ma), mb3, q_norm_g[i][None, :], kv_norm_g[i][None, :],
                               wq_x, wq_r, wk_x, wv, i, tabs, tm_b)
        o_mla = _mla_attention(qx, kx, vm, tm_b)
        h = _mix_ffn(h, o_sb.reshape(rows, SB_WIDTH), o_ssm.reshape(rows, SSM_WIDTH), o_mla.reshape(rows, MLA_WIDTH),
                     wo_b, ffn_norm_g[i][None, :], wg_b, wu_b, wd_b, i, tm, 704)
    return _final_norm(h, final_norm_g[None, :], b, seq, 1024 if seq % 1024 == 0 else BLOCK)
```

```python
import functools
import math

import jax
import jax.numpy as jnp
from jax import lax
from jax.experimental import pallas as pl
from jax.experimental.pallas import tpu as pltpu

F32 = jnp.float32
BF16 = jnp.bfloat16

D_MODEL = 1024
N_META = 16
BLOCK = 128
N_PAD = BLOCK - N_META
SB_HEADS = 8
SB_HEAD_DIM = 64
SB_WIDTH = SB_HEADS * SB_HEAD_DIM
SSM_HEADS = 16
SSM_HEAD_DIM = 64
SSM_WIDTH = SSM_HEADS * SSM_HEAD_DIM
SSM_GROUPS = 2
SSM_STATE = 128
CONV_K = 4
CONV_CH = SSM_WIDTH + 2 * SSM_GROUPS * SSM_STATE
MLA_HEADS = 8
Q_LORA = 384
KV_LORA = 256
QK_NOPE = 64
QK_ROPE = 32
V_HEAD = 64
MLA_WIDTH = MLA_HEADS * V_HEAD
ROPE_BASE = 10000.0
MIX_WIDTH = SB_WIDTH + SSM_WIDTH + MLA_WIDTH
D_FF = 2816
EPS = 1e-6

LANES = 128
ROPE_LANE0 = QK_NOPE
DT_LANE0 = QK_NOPE + QK_ROPE
MLA_MASK_LANE = QK_NOPE + QK_ROPE
HIST = 8
SB_EXIT = -160.0
NEG_BIG = -1e30
MLA_ROWS = 80
VMEM_LIMIT = 56 * 1024 * 1024

IN_MAIN = (("q", SB_WIDTH, BF16), ("k", SB_WIDTH, BF16), ("v", SB_WIDTH, BF16),
           ("z", SSM_WIDTH, F32), ("xbc", CONV_CH, F32))
IN_TAIL = (("cq", Q_LORA, F32), ("ckv", KV_LORA, F32), ("ma", LANES, F32), ("mb", LANES, F32))
IN_SEGS = IN_MAIN + IN_TAIL
IN_MAIN_WIDTH = sum(s[1] for s in IN_MAIN)
IN_TAIL_WIDTH = sum(s[1] for s in IN_TAIL)


def _cparams(*sem):
    return pltpu.CompilerParams(dimension_semantics=sem, vmem_limit_bytes=VMEM_LIMIT)


def _const_spec(shape):
    nd = len(shape)
    return pl.BlockSpec(shape, lambda *_: (0,) * nd, pipeline_mode=pl.Buffered(1))


def _layer_spec(shape, layer):
    nd = len(shape)
    return pl.BlockSpec((None,) + tuple(shape), lambda *_: (layer,) + (0,) * nd, pipeline_mode=pl.Buffered(1))


def _rms(x, g):
    return x * lax.rsqrt(jnp.mean(x * x, axis=-1, keepdims=True) + EPS) * g


def _silu(x):
    return x * (1.0 / (1.0 + jnp.exp(-x)))


def _softplus(x):
    return jnp.maximum(x, 0.0) + jnp.log(1.0 + jnp.exp(-jnp.abs(x)))


def _split3(x):
    hi = x.astype(BF16)
    r1 = x - hi.astype(F32)
    mid = r1.astype(BF16)
    lo = (r1 - mid.astype(F32)).astype(BF16)
    return hi, mid, lo


def _dot(a, b):
    return jnp.dot(a, b, preferred_element_type=F32)


def _dot_nt(a, b):
    return lax.dot_general(a, b, (((1,), (1,)), ((), ())), preferred_element_type=F32)


def _dot_f32_lhs(x, rhs01):
    return sum(_dot(p, rhs01) for p in _split3(x))


def _dot_f32_rhs(lhs01, x):
    return sum(_dot(lhs01, p) for p in _split3(x))


def _inproj_body(h_ref, g_ref, wm_ref, wt_ref, *out_refs):
    hn = _rms(h_ref[...], g_ref[...]).astype(BF16)
    for w_ref, refs in ((wm_ref, out_refs[:len(IN_MAIN)]), (wt_ref, out_refs[len(IN_MAIN):])):
        off = 0
        for ref in refs:
            width = ref.shape[-1]
            ref[...] = _dot(hn, w_ref[:, off:off + width]).astype(ref.dtype)
            off += width


def _inproj(h, g, w_main, w_tail, layer, tm):
    rows = h.shape[0]
    return pl.pallas_call(
        _inproj_body,
        grid=(rows // tm,),
        in_specs=[pl.BlockSpec((tm, D_MODEL), lambda i: (i, 0)),
                  _const_spec((1, D_MODEL)),
                  _layer_spec((D_MODEL, IN_MAIN_WIDTH), layer), _layer_spec((D_MODEL, IN_TAIL_WIDTH), layer)],
        out_specs=[pl.BlockSpec((tm, wd), lambda i: (i, 0)) for _, wd, _ in IN_SEGS],
        out_shape=[jax.ShapeDtypeStruct((rows, wd), dt) for _, wd, dt in IN_SEGS],
        compiler_params=_cparams("parallel"),
        name="inproj",
    )(h, g, w_main, w_tail)


def _sb_body(q_ref, k_ref, v_ref, o_ref):
    T = BLOCK
    NPAIR = SB_WIDTH // LANES
    NH = SB_HEADS
    i = pl.program_id(1)
    first = lax.broadcasted_iota(jnp.int32, (T, LANES), 1) < SB_HEAD_DIM
    q_rows = []
    for p in range(NPAIR):
        qp = q_ref[:, p * LANES:(p + 1) * LANES]
        qz = jnp.zeros_like(qp)
        q_rows.append(jnp.concatenate([jnp.where(first, qp, qz), jnp.where(first, qz, qp)], axis=0))
    ur = lax.broadcasted_iota(jnp.int32, (T, 2 * T), 0)
    uc = lax.broadcasted_iota(jnp.int32, (T, 2 * T), 1)
    u = jnp.where(ur >= uc, -1.0, jnp.where(uc >= T, -1.0, 0.0)).astype(BF16)
    u2 = jnp.concatenate([u, u], axis=0)

    def tile(j, c, accs, masked):
        start = pl.multiple_of(j * T, T)
        z = jnp.concatenate([_dot_nt(q_rows[p], k_ref[pl.ds(start, T), p * LANES:(p + 1) * LANES])
                             for p in range(NPAIR)], axis=0)
        nl = jnp.maximum(z, 0.0) + jnp.log2(1.0 + jnp.exp2(-jnp.abs(z)))
        if masked:
            qpos = i * T + (lax.broadcasted_iota(jnp.int32, (NH * T, T), 0) & (T - 1))
            kpos = start + lax.broadcasted_iota(jnp.int32, (NH * T, T), 1)
            keep = lambda x: jnp.where(kpos < qpos, jnp.where(kpos >= N_PAD, x, 0.0), 0.0)
            nl = keep(nl)
        nl_hi = nl.astype(BF16)
        nl_lo = (nl - nl_hi.astype(F32)).astype(BF16)
        cs = _dot(jnp.concatenate([nl_hi, nl_lo], axis=1), u2)
        w = jnp.exp2(z + c + cs[:, :T])
        if masked:
            w = keep(w)
        w = w.astype(BF16)
        new_accs = []
        for p in range(NPAIR):
            vp = v_ref[pl.ds(start, T), p * LANES:(p + 1) * LANES]
            vz = jnp.zeros_like(vp)
            v_stack = jnp.concatenate([jnp.where(first, vp, vz), jnp.where(first, vz, vp)], axis=0)
            w_pair = jnp.concatenate([w[2 * p * T:(2 * p + 1) * T], w[(2 * p + 1) * T:(2 * p + 2) * T]], axis=1)
            new_accs.append(accs[p] + _dot(w_pair, v_stack))
        c = c + cs[:, T:]
        go = (jnp.max(c) > SB_EXIT).astype(jnp.int32)
        return go, c, new_accs

    def stepper(masked):
        def body(state):
            go, c, accs = tile(state[0], state[2], list(state[3:]), masked)
            return (state[0] - 1, go, c, *accs)
        return body

    c0 = jnp.zeros((NH * T, T), F32)
    acc0 = [jnp.zeros((T, LANES), F32)] * NPAIR
    def diag_only():
        go, c, accs = tile(i, c0, acc0, True)
        return (i - 1, go, c, *accs)

    def diag_and_two():
        _, c, accs = tile(i, c0, acc0, True)
        _, c, accs = tile(i - 1, c, accs, False)
        go, c, accs = tile(i - 2, c, accs, False)
        return (i - 3, go, c, *accs)

    state = lax.cond(i >= 3, diag_and_two, diag_only)

    def two_tiles(state):
        _, c, accs = tile(state[0], state[2], list(state[3:]), False)
        go, c, accs = tile(state[0] - 1, c, accs, False)
        return (state[0] - 2, go, c, *accs)

    state = lax.while_loop(lambda s: jnp.logical_and(s[0] >= 2, s[1] > 0), two_tiles, state)
    state = lax.while_loop(lambda s: jnp.logical_and(s[0] >= 1, s[1] > 0), stepper(False), state)
    state = lax.while_loop(lambda s: jnp.logical_and(s[0] == 0, s[1] > 0), stepper(True), state)
    for p in range(NPAIR):
        o_ref[:, p * LANES:(p + 1) * LANES] = state[3 + p].astype(o_ref.dtype)


def _sb_attention(q, k, v):
    b, L, _ = q.shape
    T = BLOCK
    return pl.pallas_call(
        _sb_body,
        grid=(b, L // T),
        in_specs=[pl.BlockSpec((None, T, SB_WIDTH), lambda bi, i: (bi, i, 0)),
                  pl.BlockSpec((None, L, SB_WIDTH), lambda bi, i: (bi, 0, 0)),
                  pl.BlockSpec((None, L, SB_WIDTH), lambda bi, i: (bi, 0, 0))],
        out_specs=pl.BlockSpec((None, T, SB_WIDTH), lambda bi, i: (bi, i, 0)),
        out_shape=jax.ShapeDtypeStruct((b, L, SB_WIDTH), BF16),
        compiler_params=_cparams("parallel", "parallel"),
        name="sb_attention",
    )(q, k, v)


def _ssd_body(xbc_ref, z_ref, mb_ref, cw_ref, cb_ref, dtb_ref, af_ref, df_ref, ng_ref, o_ref, hist_ref, state_ref):
    c = pl.program_id(1)
    T = BLOCK
    GW = SSM_WIDTH // SSM_GROUPS

    @pl.when(c == 0)
    def _():
        hist_ref[0:HIST, :] = jnp.zeros((HIST, CONV_CH), F32)
        state_ref[...] = jnp.zeros_like(state_ref)

    row = c * T + lax.broadcasted_iota(jnp.int32, (T, 1), 0)
    valid_r = row >= N_PAD
    hist_ref[HIST:HIST + T, :] = jnp.where(valid_r, xbc_ref[...], 0.0)
    conv = cb_ref[...]
    for kk in range(CONV_K):
        off = HIST - (CONV_K - 1) + kk
        conv = conv + cw_ref[kk:kk + 1, :] * hist_ref[off:off + T, :]
    tail = hist_ref[T:T + HIST, :]
    hist_ref[0:HIST, :] = tail
    xc = _silu(conv)
    xs = xc[:, :SSM_WIDTH]
    bm = xc[:, SSM_WIDTH:SSM_WIDTH + SSM_GROUPS * SSM_STATE]
    cm = xc[:, SSM_WIDTH + SSM_GROUPS * SSM_STATE:]

    er = lax.broadcasted_iota(jnp.int32, (LANES, SSM_WIDTH), 0)
    ec = lax.broadcasted_iota(jnp.int32, (LANES, SSM_WIDTH), 1)
    expand = jnp.where(er - DT_LANE0 == ec // SSM_HEAD_DIM, 1.0, 0.0).astype(BF16)
    dt_c = jnp.where(valid_r, _softplus(mb_ref[...] + dtb_ref[...]), 0.0)
    tr = lax.broadcasted_iota(jnp.int32, (T, T), 0)
    tc = lax.broadcasted_iota(jnp.int32, (T, T), 1)
    lower = tr >= tc
    tri = jnp.where(lower, 1.0, 0.0).astype(BF16)
    acs = _dot_f32_rhs(tri, dt_c * af_ref[...])
    acs_last = acs[T - 1:T, :]
    x_dt = xs * _dot_f32_lhs(dt_c, expand)
    x_dt_bf = x_dt.astype(BF16)
    xd_bf = (x_dt * _dot_f32_lhs(jnp.exp(acs_last - acs), expand)).astype(BF16)
    exp_acs = _dot_f32_lhs(jnp.exp(acs), expand)
    chunk_decay = exp_acs[T - 1:T, :]

    acs_t = acs.T

    lane = lax.broadcasted_iota(jnp.int32, (T, LANES), 1)
    first = lane < SSM_HEAD_DIM
    y_parts = []
    for g in range(SSM_GROUPS):
        gs = slice(g * GW, (g + 1) * GW)
        bg = bm[:, g * SSM_STATE:(g + 1) * SSM_STATE]
        cg = cm[:, g * SSM_STATE:(g + 1) * SSM_STATE].astype(BF16)
        cb = _dot_nt(cg, bg.astype(BF16))
        prev = state_ref[:, gs]
        y_off = _dot(cg, prev.astype(BF16)) * exp_acs[:, gs]
        state_ref[:, gs] = prev * chunk_decay[:, gs] + _dot(bg.T.astype(BF16), xd_bf[:, gs])
        for pr in range(GW // LANES):
            h0 = (g * GW + pr * LANES) // SSM_HEAD_DIM
            ps = slice(g * GW + pr * LANES, g * GW + (pr + 1) * LANES)
            xp = x_dt_bf[:, ps]
            yd = []
            for h in (h0, h0 + 1):
                seg = acs[:, DT_LANE0 + h:DT_LANE0 + h + 1] - acs_t[DT_LANE0 + h:DT_LANE0 + h + 1, :]
                m = cb * jnp.exp(jnp.where(lower, seg, -jnp.inf))
                yd.append(_dot(m.astype(BF16), xp))
            y_diag = jnp.where(first, yd[0], yd[1])
            y_parts.append(y_diag + y_off[:, pr * LANES:(pr + 1) * LANES] + xs[:, ps] * df_ref[:, ps])
    y = jnp.concatenate(y_parts, axis=-1)
    o_ref[...] = _rms(y * _silu(z_ref[...]), ng_ref[...]).astype(o_ref.dtype)


def _ssd(xbc, z, mb, cw, cb, dtb_lanes, a_lanes, d_full, ng):
    b, L, _ = xbc.shape
    T = BLOCK
    return pl.pallas_call(
        _ssd_body,
        grid=(b, L // T),
        in_specs=[pl.BlockSpec((None, T, CONV_CH), lambda bi, c: (bi, c, 0)),
                  pl.BlockSpec((None, T, SSM_WIDTH), lambda bi, c: (bi, c, 0)),
                  pl.BlockSpec((None, T, LANES), lambda bi, c: (bi, c, 0)),
                  _const_spec((CONV_K, CONV_CH)), _const_spec((1, CONV_CH)),
                  _const_spec((1, LANES)), _const_spec((1, LANES)),
                  _const_spec((1, SSM_WIDTH)), _const_spec((1, SSM_WIDTH))],
        out_specs=pl.BlockSpec((None, T, SSM_WIDTH), lambda bi, c: (bi, c, 0)),
        out_shape=jax.ShapeDtypeStruct((b, L, SSM_WIDTH), BF16),
        scratch_shapes=[pltpu.VMEM((HIST + T, CONV_CH), F32),
                        pltpu.VMEM((SSM_STATE, SSM_WIDTH), F32)],
        compiler_params=_cparams("parallel", "arbitrary"),
        name="ssd",
    )(xbc, z, mb, cw, cb, dtb_lanes, a_lanes, d_full, ng)


def _mla_prep_body(cq_ref, ckv_ref, ma_ref, mb_ref, gq_ref, gkv_ref, wq_ref, wqr_ref, wk_ref, wv_ref,
                   cosq_ref, sinq_ref, cosk_ref, sink_ref, qx_ref, kx_ref, v_ref):
    scale = math.log2(math.e) / math.sqrt(QK_NOPE + QK_ROPE)
    qn = _rms(cq_ref[...], gq_ref[...]).astype(BF16)
    q = _dot(qn, wq_ref[...])
    qr = _dot(qn, wqr_ref[...])
    cosq = cosq_ref[...]
    sinq = sinq_ref[...]
    kvn = _rms(ckv_ref[...], gkv_ref[...]).astype(BF16)
    kn = _dot(kvn, wk_ref[...])
    v_ref[...] = _dot(kvn, wv_ref[...]).astype(v_ref.dtype)
    k_rope = ma_ref[...] * cosk_ref[...] + mb_ref[...] * sink_ref[...]
    tm = q.shape[0]
    lane = lax.broadcasted_iota(jnp.int32, (tm, LANES), 1)
    pos = pl.program_id(1) * tm + lax.broadcasted_iota(jnp.int32, (tm, LANES), 0)
    mask_lane = lane == MLA_MASK_LANE
    q_bias = jnp.where(mask_lane, 1.0, 0.0)
    k_bias = k_rope + jnp.where(mask_lane, jnp.where(pos < N_PAD, NEG_BIG, 0.0), 0.0)
    for h in range(MLA_HEADS):
        hs = slice(h * LANES, (h + 1) * LANES)
        qx_ref[:, hs] = ((q[:, hs] * cosq + qr[:, hs] * sinq) * scale + q_bias).astype(qx_ref.dtype)
        kx_ref[:, hs] = (kn[:, hs] + k_bias).astype(kx_ref.dtype)


def _mla_prep(cq, ckv, ma, mb, gq, gkv, wq, wqr, wk, wv, layer, tabs, tm):
    b, L, _ = cq.shape
    HW = MLA_HEADS * LANES
    row = lambda wd: pl.BlockSpec((None, tm, wd), lambda bi, i: (bi, i, 0))
    tab = pl.BlockSpec((tm, LANES), lambda bi, i: (i, 0))
    return pl.pallas_call(
        _mla_prep_body,
        grid=(b, L // tm),
        in_specs=[row(Q_LORA), row(KV_LORA), row(LANES), row(LANES),
                  _const_spec((1, Q_LORA)), _const_spec((1, KV_LORA)),
                  _layer_spec((Q_LORA, HW), layer), _layer_spec((Q_LORA, HW), layer),
                  _layer_spec((KV_LORA, HW), layer), _layer_spec((KV_LORA, MLA_WIDTH), layer),
                  tab, tab, tab, tab],
        out_specs=[row(HW), row(HW), row(MLA_WIDTH)],
        out_shape=[jax.ShapeDtypeStruct((b, L, HW), BF16), jax.ShapeDtypeStruct((b, L, HW), BF16),
                   jax.ShapeDtypeStruct((b, L, MLA_WIDTH), BF16)],
        compiler_params=_cparams("parallel", "parallel"),
        name="mla_prep",
    )(cq, ckv, ma, mb, gq, gkv, wq, wqr, wk, wv, *tabs)


def _mla_body(q_ref, k_ref, v_ref, o_ref, s_a, s_b, p_a, p_b, al_a, al_b, m_ref, l_ref, acc_ref, *, tq, n_blocks):
    tk = tq
    i = pl.program_id(2)
    n_lane_tiles = tk // LANES
    s_bufs, p_bufs, al_bufs = (s_a, s_b), (p_a, p_b), (al_a, al_b)
    first_v = lax.broadcasted_iota(jnp.int32, (tk, LANES), 1) < V_HEAD
    rc = min(MLA_ROWS, tq)
    first_c = lax.broadcasted_iota(jnp.int32, (rc, LANES), 1) < V_HEAD
    row_c = lax.broadcasted_iota(jnp.int32, (rc, tk), 0)
    col_c = lax.broadcasted_iota(jnp.int32, (rc, tk), 1)

    m_ref[...] = jnp.full(m_ref.shape, NEG_BIG, F32)
    l_ref[...] = jnp.zeros(l_ref.shape, F32)
    acc_ref[...] = jnp.zeros(acc_ref.shape, F32)

    def scores(t, slot):
        start = pl.multiple_of(t * tk, tk)
        for h in range(2):
            s_bufs[slot][h] = _dot_nt(q_ref[:, h * LANES:(h + 1) * LANES],
                                      k_ref[pl.ds(start, tk), h * LANES:(h + 1) * LANES])

    def softmax(slot, mask):
        m_old_all = [m_ref[0], m_ref[1]]
        l_old_all = [l_ref[0], l_ref[1]]
        m_out, l_out, al_out = ([], []), ([], []), []
        for r0 in range(0, tq, rc):
            rows = slice(r0, r0 + rc)
            alphas = []
            for h in range(2):
                s = s_bufs[slot][h, rows, :]
                if mask == "first":
                    s = jnp.where(col_c <= i * tq + r0 + row_c, s, NEG_BIG)
                elif mask == "diag":
                    s = jnp.where(col_c <= r0 + row_c, s, NEG_BIG)
                m_old = m_old_all[h][rows, :]
                m_new = jnp.maximum(m_old, jnp.max(s, axis=-1, keepdims=True))
                alpha = jnp.exp2(m_old - m_new)
                p = jnp.exp2(s - jnp.concatenate([m_new] * n_lane_tiles, axis=1))
                m_out[h].append(m_new)
                l_out[h].append(alpha * l_old_all[h][rows, :] + jnp.sum(p, axis=-1, keepdims=True))
                p_bufs[slot][rows, h * tk:(h + 1) * tk] = p.astype(BF16)
                alphas.append(alpha)
            al_out.append(jnp.where(first_c, alphas[0], alphas[1]))
        for h in range(2):
            m_ref[h] = jnp.concatenate(m_out[h], axis=0)
            l_ref[h] = jnp.concatenate(l_out[h], axis=0)
        al_bufs[slot][...] = jnp.concatenate(al_out, axis=0)

    def update(t, slot):
        start = pl.multiple_of(t * tk, tk)
        vt = v_ref[pl.ds(start, tk), :]
        vz = jnp.zeros_like(vt)
        v_stack = jnp.concatenate([jnp.where(first_v, vt, vz), jnp.where(first_v, vz, vt)], axis=0)
        acc_ref[...] = al_bufs[slot][...] * acc_ref[...] + _dot(p_bufs[slot][...], v_stack)

    scores(0, 0)
    if n_blocks > 1:
        scores(1, 1)
    softmax(0, "first")

    @pl.when(i == 0)
    def _():
        update(0, 0)

    @pl.when(i > 0)
    def _():
        n_plain = i - 1

        def two_tiles(u, _):
            t = 1 + 2 * u
            scores(t + 1, 0)
            update(t - 1, 0)
            softmax(1, None)
            scores(t + 2, 1)
            update(t, 1)
            softmax(0, None)
            return 0

        lax.fori_loop(0, n_plain // 2, two_tiles, 0)

        @pl.when(n_plain % 2 == 1)
        def _():
            scores(i, 0)
            update(i - 2, 0)
            softmax(1, None)

        for parity in (0, 1):
            @pl.when((i & 1) == parity)
            def _(parity=parity):
                update(i - 1, 1 - parity)
                softmax(parity, "diag")
                update(i, parity)

    first_o = lax.broadcasted_iota(jnp.int32, (tq, LANES), 1) < V_HEAD
    o_ref[...] = (acc_ref[...] / jnp.where(first_o, l_ref[0], l_ref[1])).astype(o_ref.dtype)


def _mla_attention(qx, kx, v, tq):
    b, L, _ = qx.shape
    npairs = MLA_WIDTH // LANES
    return pl.pallas_call(
        functools.partial(_mla_body, tq=tq, n_blocks=L // tq),
        grid=(b, npairs, L // tq),
        in_specs=[pl.BlockSpec((None, tq, 2 * LANES), lambda bi, p, i: (bi, i, p)),
                  pl.BlockSpec((None, L, 2 * LANES), lambda bi, p, i: (bi, 0, p)),
                  pl.BlockSpec((None, L, LANES), lambda bi, p, i: (bi, 0, p))],
        out_specs=pl.BlockSpec((None, tq, LANES), lambda bi, p, i: (bi, i, p)),
        out_shape=jax.ShapeDtypeStruct((b, L, MLA_WIDTH), BF16),
        scratch_shapes=[pltpu.VMEM((2, tq, tq), F32), pltpu.VMEM((2, tq, tq), F32),
                        pltpu.VMEM((tq, 2 * tq), BF16), pltpu.VMEM((tq, 2 * tq), BF16),
                        pltpu.VMEM((tq, LANES), F32), pltpu.VMEM((tq, LANES), F32),
                        pltpu.VMEM((2, tq, LANES), F32), pltpu.VMEM((2, tq, LANES), F32),
                        pltpu.VMEM((tq, LANES), F32)],
        compiler_params=_cparams("parallel", "parallel", "parallel"),
        name="mla_attention",
    )(qx, kx, v)


def _mix_ffn_body(h_ref, osb_ref, ossm_ref, omla_ref, wo_ref, g_ref, wg_ref, wu_ref, wd_ref, o_ref, *, ff_chunk):
    h = h_ref[...]
    h = h + _dot(osb_ref[...], wo_ref[0:SB_WIDTH, :])
    h = h + _dot(ossm_ref[...], wo_ref[SB_WIDTH:SB_WIDTH + SSM_WIDTH, :])
    h = h + _dot(omla_ref[...], wo_ref[SB_WIDTH + SSM_WIDTH:, :])
    fn = _rms(h, g_ref[...]).astype(BF16)
    o_ref[...] = h
    for f in range(0, D_FF, ff_chunk):
        gate = _dot(fn, wg_ref[:, f:f + ff_chunk])
        up = _dot(fn, wu_ref[:, f:f + ff_chunk])
        o_ref[...] += _dot((_silu(gate) * up).astype(BF16), wd_ref[f:f + ff_chunk, :])


def _mix_ffn(h, osb, ossm, omla, wo, g, wg, wu, wd, layer, tm, ff_chunk):
    rows = h.shape[0]
    row = lambda wd_: pl.BlockSpec((tm, wd_), lambda i: (i, 0))
    return pl.pallas_call(
        functools.partial(_mix_ffn_body, ff_chunk=ff_chunk),
        grid=(rows // tm,),
        in_specs=[row(D_MODEL), row(SB_WIDTH), row(SSM_WIDTH), row(MLA_WIDTH),
                  _layer_spec((MIX_WIDTH, D_MODEL), layer), _const_spec((1, D_MODEL)),
                  _layer_spec((D_MODEL, D_FF), layer), _layer_spec((D_MODEL, D_FF), layer),
                  _layer_spec((D_FF, D_MODEL), layer)],
        out_specs=row(D_MODEL),
        out_shape=jax.ShapeDtypeStruct((rows, D_MODEL), F32),
        compiler_params=_cparams("parallel"),
        name="mix_ffn",
    )(h, osb, ossm, omla, wo, g, wg, wu, wd)


def _final_norm_body(h_ref, g_ref, o_ref):
    o_ref[...] = _rms(h_ref[...], g_ref[...])


def _final_norm(h, g, b, seq, tm):
    L = h.shape[0] // b
    return pl.pallas_call(
        _final_norm_body,
        grid=(b, seq // tm),
        in_specs=[pl.BlockSpec((pl.Element(tm), pl.Element(D_MODEL)),
                               lambda bi, i: (pl.multiple_of(bi * L + i * tm + (L - seq), BLOCK), 0)),
                  _const_spec((1, D_MODEL))],
        out_specs=pl.BlockSpec((None, tm, D_MODEL), lambda bi, i: (bi, i, 0)),
        out_shape=jax.ShapeDtypeStruct((b, seq, D_MODEL), F32),
        compiler_params=_cparams("parallel", "parallel"),
        name="final_norm",
    )(h, g)


def _rot_half_cols(w):
    half = QK_ROPE // 2
    return jnp.concatenate([-w[..., half:], w[..., :half]], axis=-1)


def _lane_block(cols, lane0):
    n = cols.shape[-1]
    return jnp.pad(cols, ((0, 0),) * (cols.ndim - 1) + ((lane0, LANES - lane0 - n),))


def _prep_in_weights(w_in):
    sb_scale = math.log2(math.e) / math.sqrt(SB_HEAD_DIM)
    col_scale = jnp.concatenate([jnp.full((SB_WIDTH,), sb_scale, F32), jnp.ones((IN_MAIN_WIDTH - SB_WIDTH,), F32)])
    w_main = (w_in[..., :IN_MAIN_WIDTH] * col_scale).astype(BF16)
    o = IN_MAIN_WIDTH
    parts = {}
    for name, wd in (("dt", SSM_HEADS), ("cq", Q_LORA), ("ckv", KV_LORA), ("kr", QK_ROPE)):
        parts[name] = w_in[..., o:o + wd]
        o += wd
    ma = _lane_block(parts["kr"], ROPE_LANE0)
    mb = _lane_block(jnp.concatenate([_rot_half_cols(parts["kr"]), parts["dt"]], axis=-1), ROPE_LANE0)
    w_tail = jnp.concatenate([parts["cq"], parts["ckv"], ma, mb], axis=-1).astype(BF16)
    return w_main, w_tail


def _prep_mla_weights(w_uq, w_ukv):
    depth = w_uq.shape[0]
    dq = QK_NOPE + QK_ROPE
    wq = w_uq.reshape(depth, Q_LORA, MLA_HEADS, dq)
    pad = jnp.zeros((depth, Q_LORA, MLA_HEADS, LANES - dq), F32)
    wq_x = jnp.concatenate([wq, pad], axis=-1)
    wq_r = jnp.concatenate([jnp.zeros((depth, Q_LORA, MLA_HEADS, QK_NOPE), F32),
                            _rot_half_cols(wq[..., QK_NOPE:]), pad], axis=-1)
    wkv = w_ukv.reshape(depth, KV_LORA, MLA_HEADS, QK_NOPE + V_HEAD)
    wk_x = jnp.concatenate([wkv[..., :QK_NOPE], jnp.zeros((depth, KV_LORA, MLA_HEADS, LANES - QK_NOPE), F32)],
                           axis=-1)
    wv = wkv[..., QK_NOPE:]
    flat = lambda a: a.reshape(a.shape[0], a.shape[1], -1).astype(BF16)
    return flat(wq_x), flat(wq_r), flat(wk_x), flat(wv)


def _rope_tables(L):
    pos = (jnp.arange(L) - N_PAD).astype(F32)
    inv = ROPE_BASE ** (-jnp.arange(0, QK_ROPE, 2, dtype=F32) / QK_ROPE)
    ang = pos[:, None] * inv[None, :]
    cos2 = jnp.concatenate([jnp.cos(ang), jnp.cos(ang)], axis=-1)
    sin2 = jnp.concatenate([jnp.sin(ang), jnp.sin(ang)], axis=-1)
    tail = jnp.zeros((L, LANES - QK_NOPE - QK_ROPE), F32)
    head0 = jnp.zeros((L, QK_NOPE), F32)
    cosk = jnp.concatenate([head0, cos2, tail], axis=-1)
    sink = jnp.concatenate([head0, sin2, tail], axis=-1)
    cosq = jnp.concatenate([jnp.ones((L, QK_NOPE), F32), cos2, tail], axis=-1)
    return cosq, sink, cosk, sink


def _per_head_lanes(p):
    return jnp.repeat(p, SSM_HEAD_DIM)[None, :]


def kernel(x, meta_tokens, mix_norm_g, w_in, conv_w, conv_b, dt_bias, a_log, d_skip, ssm_norm_g, q_norm_g, w_uq,
           kv_norm_g, w_ukv, w_out, ffn_norm_g, w_gate, w_up, w_down, final_norm_g):
    b, seq, _ = x.shape
    depth = w_in.shape[0]
    L = seq + BLOCK
    rows = b * L
    tm = 640 if rows % 640 == 0 else BLOCK
    tm_b = 640 if L % 640 == 0 else BLOCK
    h = jnp.concatenate([jnp.zeros((b, N_PAD, D_MODEL), F32),
                         jnp.broadcast_to(meta_tokens[None], (b, N_META, D_MODEL)), x], axis=1).reshape(rows, D_MODEL)
    tabs = _rope_tables(L)
    w_in_main, w_in_tail = _prep_in_weights(w_in)
    wq_x, wq_r, wk_x, wv = _prep_mla_weights(w_uq, w_ukv)
    wo_b, wg_b, wu_b, wd_b = (w.astype(BF16) for w in (w_out, w_gate, w_up, w_down))
    r3 = lambda a: a.reshape(b, L, a.shape[-1])
    for i in range(depth):
        q, k, v, z, xbc, cq, ckv, ma, mb = _inproj(h, mix_norm_g[i][None, :], w_in_main, w_in_tail, i, tm)
        o_sb = _sb_attention(r3(q), r3(k), r3(v))
        a = -jnp.exp(a_log[i])
        mb3 = r3(mb)
        o_ssm = _ssd(r3(xbc), r3(z), mb3, conv_w[i], conv_b[i][None, :],
                     _lane_block(dt_bias[i][None, :], DT_LANE0), _lane_block(a[None, :], DT_LANE0),
                     _per_head_lanes(d_skip[i]), ssm_norm_g[i][None, :])
        qx, kx, vm = _mla_prep(r3(cq), r3(ckv), r3(ma), mb3, q_norm_g[i][None, :], kv_norm_g[i][None, :],
                               wq_x, wq_r, wk_x, wv, i, tabs, tm_b)
        o_mla = _mla_attention(qx, kx, vm, tm_b)
        h = _mix_ffn(h, o_sb.reshape(rows, SB_WIDTH), o_ssm.reshape(rows, SSM_WIDTH), o_mla.reshape(rows, MLA_WIDTH),
                     wo_b, ffn_norm_g[i][None, :], wg_b, wu_b, wd_b, i, tm, 704)
    return _final_norm(h, final_norm_g[None, :], b, seq, 1024 if seq % 1024 == 0 else BLOCK)
```

```python
import functools
import math

import jax
import jax.numpy as jnp
from jax import lax
from jax.experimental import pallas as pl
from jax.experimental.pallas import tpu as pltpu

F32 = jnp.float32
BF16 = jnp.bfloat16

D_MODEL = 1024
N_META = 16
BLOCK = 128
N_PAD = BLOCK - N_META
SB_HEADS = 8
SB_HEAD_DIM = 64
SB_WIDTH = SB_HEADS * SB_HEAD_DIM
SSM_HEADS = 16
SSM_HEAD_DIM = 64
SSM_WIDTH = SSM_HEADS * SSM_HEAD_DIM
SSM_GROUPS = 2
SSM_STATE = 128
CONV_K = 4
CONV_CH = SSM_WIDTH + 2 * SSM_GROUPS * SSM_STATE
MLA_HEADS = 8
Q_LORA = 384
KV_LORA = 256
QK_NOPE = 64
QK_ROPE = 32
V_HEAD = 64
MLA_WIDTH = MLA_HEADS * V_HEAD
ROPE_BASE = 10000.0
MIX_WIDTH = SB_WIDTH + SSM_WIDTH + MLA_WIDTH
D_FF = 2816
EPS = 1e-6

LANES = 128
ROPE_LANE0 = QK_NOPE
DT_LANE0 = QK_NOPE + QK_ROPE
MLA_MASK_LANE = QK_NOPE + QK_ROPE
HIST = 8
SB_EXIT = -160.0
NEG_BIG = -1e30
MLA_ROWS = 80
VMEM_LIMIT = 56 * 1024 * 1024

IN_MAIN = (("q", SB_WIDTH, BF16), ("k", SB_WIDTH, BF16), ("v", SB_WIDTH, BF16),
           ("z", SSM_WIDTH, F32), ("xbc", CONV_CH, F32))
IN_TAIL = (("cq", Q_LORA, F32), ("ckv", KV_LORA, F32), ("ma", LANES, F32), ("mb", LANES, F32))
IN_SEGS = IN_MAIN + IN_TAIL
IN_MAIN_WIDTH = sum(s[1] for s in IN_MAIN)
IN_TAIL_WIDTH = sum(s[1] for s in IN_TAIL)


def _cparams(*sem):
    return pltpu.CompilerParams(dimension_semantics=sem, vmem_limit_bytes=VMEM_LIMIT)


def _const_spec(shape):
    nd = len(shape)
    return pl.BlockSpec(shape, lambda *_: (0,) * nd, pipeline_mode=pl.Buffered(1))


def _layer_spec(shape, layer):
    nd = len(shape)
    return pl.BlockSpec((None,) + tuple(shape), lambda *_: (layer,) + (0,) * nd, pipeline_mode=pl.Buffered(1))


def _rms(x, g):
    return x * lax.rsqrt(jnp.mean(x * x, axis=-1, keepdims=True) + EPS) * g


def _silu(x):
    return x * (1.0 / (1.0 + jnp.exp(-x)))


def _softplus(x):
    return jnp.maximum(x, 0.0) + jnp.log(1.0 + jnp.exp(-jnp.abs(x)))


def _split3(x):
    hi = x.astype(BF16)
    r1 = x - hi.astype(F32)
    mid = r1.astype(BF16)
    lo = (r1 - mid.astype(F32)).astype(BF16)
    return hi, mid, lo


def _dot(a, b):
    return jnp.dot(a, b, preferred_element_type=F32)


def _dot_nt(a, b):
    return lax.dot_general(a, b, (((1,), (1,)), ((), ())), preferred_element_type=F32)


def _dot_f32_lhs(x, rhs01):
    return sum(_dot(p, rhs01) for p in _split3(x))


def _dot_f32_rhs(lhs01, x):
    return sum(_dot(lhs01, p) for p in _split3(x))


def _inproj_body(h_ref, g_ref, wm_ref, wt_ref, *out_refs):
    hn = _rms(h_ref[...], g_ref[...]).astype(BF16)
    for w_ref, refs in ((wm_ref, out_refs[:len(IN_MAIN)]), (wt_ref, out_refs[len(IN_MAIN):])):
        off = 0
        for ref in refs:
            width = ref.shape[-1]
            ref[...] = _dot(hn, w_ref[:, off:off + width]).astype(ref.dtype)
            off += width


def _inproj(h, g, w_main, w_tail, layer, tm):
    rows = h.shape[0]
    return pl.pallas_call(
        _inproj_body,
        grid=(rows // tm,),
        in_specs=[pl.BlockSpec((tm, D_MODEL), lambda i: (i, 0)),
                  _const_spec((1, D_MODEL)),
                  _layer_spec((D_MODEL, IN_MAIN_WIDTH), layer), _layer_spec((D_MODEL, IN_TAIL_WIDTH), layer)],
        out_specs=[pl.BlockSpec((tm, wd), lambda i: (i, 0)) for _, wd, _ in IN_SEGS],
        out_shape=[jax.ShapeDtypeStruct((rows, wd), dt) for _, wd, dt in IN_SEGS],
        compiler_params=_cparams("parallel"),
        name="inproj",
    )(h, g, w_main, w_tail)


def _sb_body(q_ref, k_ref, v_ref, o_ref):
    T = BLOCK
    NPAIR = SB_WIDTH // LANES
    NH = SB_HEADS
    i = pl.program_id(1)
    first = lax.broadcasted_iota(jnp.int32, (T, LANES), 1) < SB_HEAD_DIM
    q_rows = []
    for p in range(NPAIR):
        qp = q_ref[:, p * LANES:(p + 1) * LANES]
        qz = jnp.zeros_like(qp)
        q_rows.append(jnp.concatenate([jnp.where(first, qp, qz), jnp.where(first, qz, qp)], axis=0))
    ur = lax.broadcasted_iota(jnp.int32, (T, 2 * T), 0)
    uc = lax.broadcasted_iota(jnp.int32, (T, 2 * T), 1)
    u = jnp.where(ur >= uc, -1.0, jnp.where(uc >= T, -1.0, 0.0)).astype(BF16)
    u2 = jnp.concatenate([u, u], axis=0)

    def tile(j, c, accs, masked):
        start = pl.multiple_of(j * T, T)
        z = jnp.concatenate([_dot_nt(q_rows[p], k_ref[pl.ds(start, T), p * LANES:(p + 1) * LANES])
                             for p in range(NPAIR)], axis=0)
        nl = jnp.maximum(z, 0.0) + jnp.log2(1.0 + jnp.exp2(-jnp.abs(z)))
        if masked:
            qpos = i * T + (lax.broadcasted_iota(jnp.int32, (NH * T, T), 0) & (T - 1))
            kpos = start + lax.broadcasted_iota(jnp.int32, (NH * T, T), 1)
            keep = lambda x: jnp.where(kpos < qpos, jnp.where(kpos >= N_PAD, x, 0.0), 0.0)
            nl = keep(nl)
        nl_hi = nl.astype(BF16)
        nl_lo = (nl - nl_hi.astype(F32)).astype(BF16)
        cs = _dot(jnp.concatenate([nl_hi, nl_lo], axis=1), u2)
        w = jnp.exp2(z + c + cs[:, :T])
        if masked:
            w = keep(w)
        w = w.astype(BF16)
        new_accs = []
        for p in range(NPAIR):
            vp = v_ref[pl.ds(start, T), p * LANES:(p + 1) * LANES]
            vz = jnp.zeros_like(vp)
            v_stack = jnp.concatenate([jnp.where(first, vp, vz), jnp.where(first, vz, vp)], axis=0)
            w_pair = jnp.concatenate([w[2 * p * T:(2 * p + 1) * T], w[(2 * p + 1) * T:(2 * p + 2) * T]], axis=1)
            new_accs.append(accs[p] + _dot(w_pair, v_stack))
        c = c + cs[:, T:]
        go = (jnp.max(c) > SB_EXIT).astype(jnp.int32)
        return go, c, new_accs

    def stepper(masked):
        def body(state):
            go, c, accs = tile(state[0], state[2], list(state[3:]), masked)
            return (state[0] - 1, go, c, *accs)
        return body

    c0 = jnp.zeros((NH * T, T), F32)
    acc0 = [jnp.zeros((T, LANES), F32)] * NPAIR
    def diag_only():
        go, c, accs = tile(i, c0, acc0, True)
        return (i - 1, go, c, *accs)

    def diag_and_two():
        _, c, accs = tile(i, c0, acc0, True)
        _, c, accs = tile(i - 1, c, accs, False)
        go, c, accs = tile(i - 2, c, accs, False)
        return (i - 3, go, c, *accs)

    state = lax.cond(i >= 3, diag_and_two, diag_only)

    def two_tiles(state):
        _, c, accs = tile(state[0], state[2], list(state[3:]), False)
        go, c, accs = tile(state[0] - 1, c, accs, False)
        return (state[0] - 2, go, c, *accs)

    state = lax.while_loop(lambda s: jnp.logical_and(s[0] >= 2, s[1] > 0), two_tiles, state)
    state = lax.while_loop(lambda s: jnp.logical_and(s[0] >= 1, s[1] > 0), stepper(False), state)
    state = lax.while_loop(lambda s: jnp.logical_and(s[0] == 0, s[1] > 0), stepper(True), state)
    for p in range(NPAIR):
        o_ref[:, p * LANES:(p + 1) * LANES] = state[3 + p].astype(o_ref.dtype)


def _sb_attention(q, k, v):
    b, L, _ = q.shape
    T = BLOCK
    return pl.pallas_call(
        _sb_body,
        grid=(b, L // T),
        in_specs=[pl.BlockSpec((None, T, SB_WIDTH), lambda bi, i: (bi, i, 0)),
                  pl.BlockSpec((None, L, SB_WIDTH), lambda bi, i: (bi, 0, 0)),
                  pl.BlockSpec((None, L, SB_WIDTH), lambda bi, i: (bi, 0, 0))],
        out_specs=pl.BlockSpec((None, T, SB_WIDTH), lambda bi, i: (bi, i, 0)),
        out_shape=jax.ShapeDtypeStruct((b, L, SB_WIDTH), BF16),
        compiler_params=_cparams("parallel", "parallel"),
        name="sb_attention",
    )(q, k, v)


def _ssd_body(xbc_ref, z_ref, mb_ref, cw_ref, cb_ref, dtb_ref, af_ref, df_ref, ng_ref, o_ref, hist_ref, state_ref):
    c = pl.program_id(1)
    T = BLOCK
    GW = SSM_WIDTH // SSM_GROUPS

    @pl.when(c == 0)
    def _():
        hist_ref[0:HIST, :] = jnp.zeros((HIST, CONV_CH), F32)
        state_ref[...] = jnp.zeros_like(state_ref)

    row = c * T + lax.broadcasted_iota(jnp.int32, (T, 1), 0)
    valid_r = row >= N_PAD
    hist_ref[HIST:HIST + T, :] = jnp.where(valid_r, xbc_ref[...], 0.0)
    conv = cb_ref[...]
    for kk in range(CONV_K):
        off = HIST - (CONV_K - 1) + kk
        conv = conv + cw_ref[kk:kk + 1, :] * hist_ref[off:off + T, :]
    tail = hist_ref[T:T + HIST, :]
    hist_ref[0:HIST, :] = tail
    xc = _silu(conv)
    xs = xc[:, :SSM_WIDTH]
    bm = xc[:, SSM_WIDTH:SSM_WIDTH + SSM_GROUPS * SSM_STATE]
    cm = xc[:, SSM_WIDTH + SSM_GROUPS * SSM_STATE:]

    er = lax.broadcasted_iota(jnp.int32, (LANES, SSM_WIDTH), 0)
    ec = lax.broadcasted_iota(jnp.int32, (LANES, SSM_WIDTH), 1)
    expand = jnp.where(er - DT_LANE0 == ec // SSM_HEAD_DIM, 1.0, 0.0).astype(BF16)
    dt_c = jnp.where(valid_r, _softplus(mb_ref[...] + dtb_ref[...]), 0.0)
    tr = lax.broadcasted_iota(jnp.int32, (T, T), 0)
    tc = lax.broadcasted_iota(jnp.int32, (T, T), 1)
    lower = tr >= tc
    tri = jnp.where(lower, 1.0, 0.0).astype(BF16)
    acs = _dot_f32_rhs(tri, dt_c * af_ref[...])
    acs_last = acs[T - 1:T, :]
    x_dt = xs * _dot_f32_lhs(dt_c, expand)
    x_dt_bf = x_dt.astype(BF16)
    xd_bf = (x_dt * _dot_f32_lhs(jnp.exp(acs_last - acs), expand)).astype(BF16)
    exp_acs = _dot_f32_lhs(jnp.exp(acs), expand)
    chunk_decay = exp_acs[T - 1:T, :]

    acs_t = acs.T

    lane = lax.broadcasted_iota(jnp.int32, (T, LANES), 1)
    first = lane < SSM_HEAD_DIM
    y_parts = []
    for g in range(SSM_GROUPS):
        gs = slice(g * GW, (g + 1) * GW)
        bg = bm[:, g * SSM_STATE:(g + 1) * SSM_STATE]
        cg = cm[:, g * SSM_STATE:(g + 1) * SSM_STATE].astype(BF16)
        cb = _dot_nt(cg, bg.astype(BF16))
        prev = state_ref[:, gs]
        y_off = _dot(cg, prev.astype(BF16)) * exp_acs[:, gs]
        state_ref[:, gs] = prev * chunk_decay[:, gs] + _dot(bg.T.astype(BF16), xd_bf[:, gs])
        for pr in range(GW // LANES):
            h0 = (g * GW + pr * LANES) // SSM_HEAD_DIM
            ps = slice(g * GW + pr * LANES, g * GW + (pr + 1) * LANES)
            xp = x_dt_bf[:, ps]
            yd = []
            for h in (h0, h0 + 1):
                seg = acs[:, DT_LANE0 + h:DT_LANE0 + h + 1] - acs_t[DT_LANE0 + h:DT_LANE0 + h + 1, :]
                m = cb * jnp.exp(jnp.where(lower, seg, -jnp.inf))
                yd.append(_dot(m.astype(BF16), xp))
            y_diag = jnp.where(first, yd[0], yd[1])
            y_parts.append(y_diag + y_off[:, pr * LANES:(pr + 1) * LANES] + xs[:, ps] * df_ref[:, ps])
    y = jnp.concatenate(y_parts, axis=-1)
    o_ref[...] = _rms(y * _silu(z_ref[...]), ng_ref[...]).astype(o_ref.dtype)


def _ssd(xbc, z, mb, cw, cb, dtb_lanes, a_lanes, d_full, ng):
    b, L, _ = xbc.shape
    T = BLOCK
    return pl.pallas_call(
        _ssd_body,
        grid=(b, L // T),
        in_specs=[pl.BlockSpec((None, T, CONV_CH), lambda bi, c: (bi, c, 0)),
                  pl.BlockSpec((None, T, SSM_WIDTH), lambda bi, c: (bi, c, 0)),
                  pl.BlockSpec((None, T, LANES), lambda bi, c: (bi, c, 0)),
                  _const_spec((CONV_K, CONV_CH)), _const_spec((1, CONV_CH)),
                  _const_spec((1, LANES)), _const_spec((1, LANES)),
                  _const_spec((1, SSM_WIDTH)), _const_spec((1, SSM_WIDTH))],
        out_specs=pl.BlockSpec((None, T, SSM_WIDTH), lambda bi, c: (bi, c, 0)),
        out_shape=jax.ShapeDtypeStruct((b, L, SSM_WIDTH), BF16),
        scratch_shapes=[pltpu.VMEM((HIST + T, CONV_CH), F32),
                        pltpu.VMEM((SSM_STATE, SSM_WIDTH), F32)],
        compiler_params=_cparams("parallel", "arbitrary"),
        name="ssd",
    )(xbc, z, mb, cw, cb, dtb_lanes, a_lanes, d_full, ng)


def _mla_prep_body(cq_ref, ckv_ref, ma_ref, mb_ref, gq_ref, gkv_ref, wq_ref, wqr_ref, wk_ref, wv_ref,
                   cosq_ref, sinq_ref, cosk_ref, sink_ref, qx_ref, kx_ref, v_ref):
    scale = math.log2(math.e) / math.sqrt(QK_NOPE + QK_ROPE)
    qn = _rms(cq_ref[...], gq_ref[...]).astype(BF16)
    q = _dot(qn, wq_ref[...])
    qr = _dot(qn, wqr_ref[...])
    cosq = cosq_ref[...]
    sinq = sinq_ref[...]
    kvn = _rms(ckv_ref[...], gkv_ref[...]).astype(BF16)
    kn = _dot(kvn, wk_ref[...])
    v_ref[...] = _dot(kvn, wv_ref[...]).astype(v_ref.dtype)
    k_rope = ma_ref[...] * cosk_ref[...] + mb_ref[...] * sink_ref[...]
    tm = q.shape[0]
    lane = lax.broadcasted_iota(jnp.int32, (tm, LANES), 1)
    pos = pl.program_id(1) * tm + lax.broadcasted_iota(jnp.int32, (tm, LANES), 0)
    mask_lane = lane == MLA_MASK_LANE
    q_bias = jnp.where(mask_lane, 1.0, 0.0)
    k_bias = k_rope + jnp.where(mask_lane, jnp.where(pos < N_PAD, NEG_BIG, 0.0), 0.0)
    for h in range(MLA_HEADS):
        hs = slice(h * LANES, (h + 1) * LANES)
        qx_ref[:, hs] = ((q[:, hs] * cosq + qr[:, hs] * sinq) * scale + q_bias).astype(qx_ref.dtype)
        kx_ref[:, hs] = (kn[:, hs] + k_bias).astype(kx_ref.dtype)


def _mla_prep(cq, ckv, ma, mb, gq, gkv, wq, wqr, wk, wv, layer, tabs, tm):
    b, L, _ = cq.shape
    HW = MLA_HEADS * LANES
    row = lambda wd: pl.BlockSpec((None, tm, wd), lambda bi, i: (bi, i, 0))
    tab = pl.BlockSpec((tm, LANES), lambda bi, i: (i, 0))
    return pl.pallas_call(
        _mla_prep_body,
        grid=(b, L // tm),
        in_specs=[row(Q_LORA), row(KV_LORA), row(LANES), row(LANES),
                  _const_spec((1, Q_LORA)), _const_spec((1, KV_LORA)),
                  _layer_spec((Q_LORA, HW), layer), _layer_spec((Q_LORA, HW), layer),
                  _layer_spec((KV_LORA, HW), layer), _layer_spec((KV_LORA, MLA_WIDTH), layer),
                  tab, tab, tab, tab],
        out_specs=[row(HW), row(HW), row(MLA_WIDTH)],
        out_shape=[jax.ShapeDtypeStruct((b, L, HW), BF16), jax.ShapeDtypeStruct((b, L, HW), BF16),
                   jax.ShapeDtypeStruct((b, L, MLA_WIDTH), BF16)],
        compiler_params=_cparams("parallel", "parallel"),
        name="mla_prep",
    )(cq, ckv, ma, mb, gq, gkv, wq, wqr, wk, wv, *tabs)


def _mla_body(q_ref, k_ref, v_ref, o_ref, s_a, s_b, p_a, p_b, al_a, al_b, m_ref, l_ref, acc_ref, *, tq, n_blocks):
    tk = tq
    i = pl.program_id(2)
    n_lane_tiles = tk // LANES
    s_bufs, p_bufs, al_bufs = (s_a, s_b), (p_a, p_b), (al_a, al_b)
    first_v = lax.broadcasted_iota(jnp.int32, (tk, LANES), 1) < V_HEAD
    rc = min(MLA_ROWS, tq)
    first_c = lax.broadcasted_iota(jnp.int32, (rc, LANES), 1) < V_HEAD
    row_c = lax.broadcasted_iota(jnp.int32, (rc, tk), 0)
    col_c = lax.broadcasted_iota(jnp.int32, (rc, tk), 1)

    m_ref[...] = jnp.full(m_ref.shape, NEG_BIG, F32)
    l_ref[...] = jnp.zeros(l_ref.shape, F32)
    acc_ref[...] = jnp.zeros(acc_ref.shape, F32)

    def scores(t, slot):
        start = pl.multiple_of(t * tk, tk)
        for h in range(2):
            s_bufs[slot][h] = _dot_nt(q_ref[:, h * LANES:(h + 1) * LANES],
                                      k_ref[pl.ds(start, tk), h * LANES:(h + 1) * LANES])

    def softmax(slot, mask):
        m_old_all = [m_ref[0], m_ref[1]]
        l_old_all = [l_ref[0], l_ref[1]]
        m_out, l_out, al_out = ([], []), ([], []), []
        for r0 in range(0, tq, rc):
            rows = slice(r0, r0 + rc)
            alphas = []
            for h in range(2):
                s = s_bufs[slot][h, rows, :]
                if mask == "first":
                    s = jnp.where(col_c <= i * tq + r0 + row_c, s, NEG_BIG)
                elif mask == "diag":
                    s = jnp.where(col_c <= r0 + row_c, s, NEG_BIG)
                m_old = m_old_all[h][rows, :]
                m_new = jnp.maximum(m_old, jnp.max(s, axis=-1, keepdims=True))
                alpha = jnp.exp2(m_old - m_new)
                p = jnp.exp2(s - jnp.concatenate([m_new] * n_lane_tiles, axis=1))
                m_out[h].append(m_new)
                l_out[h].append(alpha * l_old_all[h][rows, :] + jnp.sum(p, axis=-1, keepdims=True))
                p_bufs[slot][rows, h * tk:(h + 1) * tk] = p.astype(BF16)
                alphas.append(alpha)
            al_out.append(jnp.where(first_c, alphas[0], alphas[1]))
        for h in range(2):
            m_ref[h] = jnp.concatenate(m_out[h], axis=0)
            l_ref[h] = jnp.concatenate(l_out[h], axis=0)
        al_bufs[slot][...] = jnp.concatenate(al_out, axis=0)

    def update(t, slot):
        start = pl.multiple_of(t * tk, tk)
        vt = v_ref[pl.ds(start, tk), :]
        vz = jnp.zeros_like(vt)
        v_stack = jnp.concatenate([jnp.where(first_v, vt, vz), jnp.where(first_v, vz, vt)], axis=0)
        acc_ref[...] = al_bufs[slot][...] * acc_ref[...] + _dot(p_bufs[slot][...], v_stack)

    scores(0, 0)
    if n_blocks > 1:
        scores(1, 1)
    softmax(0, "first")

    @pl.when(i == 0)
    def _():
        update(0, 0)

    @pl.when(i > 0)
    def _():
        n_plain = i - 1

        def two_tiles(u, _):
            t = 1 + 2 * u
            scores(t + 1, 0)
            update(t - 1, 0)
            softmax(1, None)
            scores(t + 2, 1)
            update(t, 1)
            softmax(0, None)
            return 0

        lax.fori_loop(0, n_plain // 2, two_tiles, 0)

        @pl.when(n_plain % 2 == 1)
        def _():
            scores(i, 0)
            update(i - 2, 0)
            softmax(1, None)

        for parity in (0, 1):
            @pl.when((i & 1) == parity)
            def _(parity=parity):
                update(i - 1, 1 - parity)
                softmax(parity, "diag")
                update(i, parity)

    first_o = lax.broadcasted_iota(jnp.int32, (tq, LANES), 1) < V_HEAD
    o_ref[...] = (acc_ref[...] / jnp.where(first_o, l_ref[0], l_ref[1])).astype(o_ref.dtype)


def _mla_attention(qx, kx, v, tq):
    b, L, _ = qx.shape
    npairs = MLA_WIDTH // LANES
    return pl.pallas_call(
        functools.partial(_mla_body, tq=tq, n_blocks=L // tq),
        grid=(b, npairs, L // tq),
        in_specs=[pl.BlockSpec((None, tq, 2 * LANES), lambda bi, p, i: (bi, i, p)),
                  pl.BlockSpec((None, L, 2 * LANES), lambda bi, p, i: (bi, 0, p)),
                  pl.BlockSpec((None, L, LANES), lambda bi, p, i: (bi, 0, p))],
        out_specs=pl.BlockSpec((None, tq, LANES), lambda bi, p, i: (bi, i, p)),
        out_shape=jax.ShapeDtypeStruct((b, L, MLA_WIDTH), BF16),
        scratch_shapes=[pltpu.VMEM((2, tq, tq), F32), pltpu.VMEM((2, tq, tq), F32),
                        pltpu.VMEM((tq, 2 * tq), BF16), pltpu.VMEM((tq, 2 * tq), BF16),
                        pltpu.VMEM((tq, LANES), F32), pltpu.VMEM((tq, LANES), F32),
                        pltpu.VMEM((2, tq, LANES), F32), pltpu.VMEM((2, tq, LANES), F32),
                        pltpu.VMEM((tq, LANES), F32)],
        compiler_params=_cparams("parallel", "parallel", "parallel"),
        name="mla_attention",
    )(qx, kx, v)


def _mix_ffn_body(h_ref, osb_ref, ossm_ref, omla_ref, wo_ref, g_ref, wg_ref, wu_ref, wd_ref, o_ref):
    h = h_ref[...]
    h = h + _dot(osb_ref[...], wo_ref[0:SB_WIDTH, :])
    h = h + _dot(ossm_ref[...], wo_ref[SB_WIDTH:SB_WIDTH + SSM_WIDTH, :])
    h = h + _dot(omla_ref[...], wo_ref[SB_WIDTH + SSM_WIDTH:, :])
    fn = _rms(h, g_ref[...]).astype(BF16)
    o_ref[...] = h
    hidden = (_silu(_dot(fn, wg_ref[...])) * _dot(fn, wu_ref[...])).astype(BF16)
    o_ref[...] += _dot(hidden, wd_ref[...])


def _mix_ffn(h, osb, ossm, omla, wo, g, wg, wu, wd, layer, tm):
    rows = h.shape[0]
    row = lambda wd_: pl.BlockSpec((tm, wd_), lambda i: (i, 0))
    return pl.pallas_call(
        _mix_ffn_body,
        grid=(rows // tm,),
        in_specs=[row(D_MODEL), row(SB_WIDTH), row(SSM_WIDTH), row(MLA_WIDTH),
                  _layer_spec((MIX_WIDTH, D_MODEL), layer), _const_spec((1, D_MODEL)),
                  _layer_spec((D_MODEL, D_FF), layer), _layer_spec((D_MODEL, D_FF), layer),
                  _layer_spec((D_FF, D_MODEL), layer)],
        out_specs=row(D_MODEL),
        out_shape=jax.ShapeDtypeStruct((rows, D_MODEL), F32),
        compiler_params=_cparams("parallel"),
        name="mix_ffn",
    )(h, osb, ossm, omla, wo, g, wg, wu, wd)


def _final_norm_body(h_ref, g_ref, o_ref):
    o_ref[...] = _rms(h_ref[...], g_ref[...])


def _final_norm(h, g, b, seq, tm):
    L = h.shape[0] // b
    return pl.pallas_call(
        _final_norm_body,
        grid=(b, seq // tm),
        in_specs=[pl.BlockSpec((pl.Element(tm), pl.Element(D_MODEL)),
                               lambda bi, i: (pl.multiple_of(bi * L + i * tm + (L - seq), BLOCK), 0)),
                  _const_spec((1, D_MODEL))],
        out_specs=pl.BlockSpec((None, tm, D_MODEL), lambda bi, i: (bi, i, 0)),
        out_shape=jax.ShapeDtypeStruct((b, seq, D_MODEL), F32),
        compiler_params=_cparams("parallel", "parallel"),
        name="final_norm",
    )(h, g)


def _rot_half_cols(w):
    half = QK_ROPE // 2
    return jnp.concatenate([-w[..., half:], w[..., :half]], axis=-1)


def _lane_block(cols, lane0):
    n = cols.shape[-1]
    return jnp.pad(cols, ((0, 0),) * (cols.ndim - 1) + ((lane0, LANES - lane0 - n),))


def _prep_in_weights(w_in):
    sb_scale = math.log2(math.e) / math.sqrt(SB_HEAD_DIM)
    col_scale = jnp.concatenate([jnp.full((SB_WIDTH,), sb_scale, F32), jnp.ones((IN_MAIN_WIDTH - SB_WIDTH,), F32)])
    w_main = (w_in[..., :IN_MAIN_WIDTH] * col_scale).astype(BF16)
    o = IN_MAIN_WIDTH
    parts = {}
    for name, wd in (("dt", SSM_HEADS), ("cq", Q_LORA), ("ckv", KV_LORA), ("kr", QK_ROPE)):
        parts[name] = w_in[..., o:o + wd]
        o += wd
    ma = _lane_block(parts["kr"], ROPE_LANE0)
    mb = _lane_block(jnp.concatenate([_rot_half_cols(parts["kr"]), parts["dt"]], axis=-1), ROPE_LANE0)
    w_tail = jnp.concatenate([parts["cq"], parts["ckv"], ma, mb], axis=-1).astype(BF16)
    return w_main, w_tail


def _prep_mla_weights(w_uq, w_ukv):
    depth = w_uq.shape[0]
    dq = QK_NOPE + QK_ROPE
    wq = w_uq.reshape(depth, Q_LORA, MLA_HEADS, dq)
    pad = jnp.zeros((depth, Q_LORA, MLA_HEADS, LANES - dq), F32)
    wq_x = jnp.concatenate([wq, pad], axis=-1)
    wq_r = jnp.concatenate([jnp.zeros((depth, Q_LORA, MLA_HEADS, QK_NOPE), F32),
                            _rot_half_cols(wq[..., QK_NOPE:]), pad], axis=-1)
    wkv = w_ukv.reshape(depth, KV_LORA, MLA_HEADS, QK_NOPE + V_HEAD)
    wk_x = jnp.concatenate([wkv[..., :QK_NOPE], jnp.zeros((depth, KV_LORA, MLA_HEADS, LANES - QK_NOPE), F32)],
                           axis=-1)
    wv = wkv[..., QK_NOPE:]
    flat = lambda a: a.reshape(a.shape[0], a.shape[1], -1).astype(BF16)
    return flat(wq_x), flat(wq_r), flat(wk_x), flat(wv)


def _rope_tables(L):
    pos = (jnp.arange(L) - N_PAD).astype(F32)
    inv = ROPE_BASE ** (-jnp.arange(0, QK_ROPE, 2, dtype=F32) / QK_ROPE)
    ang = pos[:, None] * inv[None, :]
    cos2 = jnp.concatenate([jnp.cos(ang), jnp.cos(ang)], axis=-1)
    sin2 = jnp.concatenate([jnp.sin(ang), jnp.sin(ang)], axis=-1)
    tail = jnp.zeros((L, LANES - QK_NOPE - QK_ROPE), F32)
    head0 = jnp.zeros((L, QK_NOPE), F32)
    cosk = jnp.concatenate([head0, cos2, tail], axis=-1)
    sink = jnp.concatenate([head0, sin2, tail], axis=-1)
    cosq = jnp.concatenate([jnp.ones((L, QK_NOPE), F32), cos2, tail], axis=-1)
    return cosq, sink, cosk, sink


def _per_head_lanes(p):
    return jnp.repeat(p, SSM_HEAD_DIM)[None, :]


def kernel(x, meta_tokens, mix_norm_g, w_in, conv_w, conv_b, dt_bias, a_log, d_skip, ssm_norm_g, q_norm_g, w_uq,
           kv_norm_g, w_ukv, w_out, ffn_norm_g, w_gate, w_up, w_down, final_norm_g):
    b, seq, _ = x.shape
    depth = w_in.shape[0]
    L = seq + BLOCK
    rows = b * L
    tm = 640 if rows % 640 == 0 else BLOCK
    tm_b = 640 if L % 640 == 0 else BLOCK
    h = jnp.concatenate([jnp.zeros((b, N_PAD, D_MODEL), F32),
                         jnp.broadcast_to(meta_tokens[None], (b, N_META, D_MODEL)), x], axis=1).reshape(rows, D_MODEL)
    tabs = _rope_tables(L)
    w_in_main, w_in_tail = _prep_in_weights(w_in)
    wq_x, wq_r, wk_x, wv = _prep_mla_weights(w_uq, w_ukv)
    wo_b, wg_b, wu_b, wd_b = (w.astype(BF16) for w in (w_out, w_gate, w_up, w_down))
    r3 = lambda a: a.reshape(b, L, a.shape[-1])
    for i in range(depth):
        q, k, v, z, xbc, cq, ckv, ma, mb = _inproj(h, mix_norm_g[i][None, :], w_in_main, w_in_tail, i, tm)
        o_sb = _sb_attention(r3(q), r3(k), r3(v))
        a = -jnp.exp(a_log[i])
        mb3 = r3(mb)
        o_ssm = _ssd(r3(xbc), r3(z), mb3, conv_w[i], conv_b[i][None, :],
                     _lane_block(dt_bias[i][None, :], DT_LANE0), _lane_block(a[None, :], DT_LANE0),
                     _per_head_lanes(d_skip[i]), ssm_norm_g[i][None, :])
        qx, kx, vm = _mla_prep(r3(cq), r3(ckv), r3(ma), mb3, q_norm_g[i][None, :], kv_norm_g[i][None, :],
                               wq_x, wq_r, wk_x, wv, i, tabs, tm_b)
        o_mla = _mla_attention(qx, kx, vm, tm_b)
        h = _mix_ffn(h, o_sb.reshape(rows, SB_WIDTH), o_ssm.reshape(rows, SSM_WIDTH), o_mla.reshape(rows, MLA_WIDTH),
                     wo_b, ffn_norm_g[i][None, :], wg_b, wu_b, wd_b, i, tm)
    return _final_norm(h, final_norm_g[None, :], b, seq, 1024 if seq % 1024 == 0 else BLOCK)
```

```python
import functools
import math

import jax
import jax.numpy as jnp
from jax import lax
from jax.experimental import pallas as pl
from jax.experimental.pallas import tpu as pltpu

F32 = jnp.float32
BF16 = jnp.bfloat16

D_MODEL = 1024
N_META = 16
BLOCK = 128
N_PAD = BLOCK - N_META
SB_HEADS = 8
SB_HEAD_DIM = 64
SB_WIDTH = SB_HEADS * SB_HEAD_DIM
SSM_HEADS = 16
SSM_HEAD_DIM = 64
SSM_WIDTH = SSM_HEADS * SSM_HEAD_DIM
SSM_GROUPS = 2
SSM_STATE = 128
CONV_K = 4
CONV_CH = SSM_WIDTH + 2 * SSM_GROUPS * SSM_STATE
MLA_HEADS = 8
Q_LORA = 384
KV_LORA = 256
QK_NOPE = 64
QK_ROPE = 32
V_HEAD = 64
MLA_WIDTH = MLA_HEADS * V_HEAD
ROPE_BASE = 10000.0
MIX_WIDTH = SB_WIDTH + SSM_WIDTH + MLA_WIDTH
D_FF = 2816
EPS = 1e-6

LANES = 128
ROPE_LANE0 = QK_NOPE
DT_LANE0 = QK_NOPE + QK_ROPE
MLA_MASK_LANE = QK_NOPE + QK_ROPE
HIST = 8
SB_EXIT = -160.0
NEG_BIG = -1e30
MLA_ROWS = 80
VMEM_LIMIT = 56 * 1024 * 1024
ROW_TILE = 5 * BLOCK
NORM_TILE = 8 * BLOCK


def _tile(n, preferred):
    return preferred if n % preferred == 0 else BLOCK

IN_MAIN = (("q", SB_WIDTH, BF16), ("k", SB_WIDTH, BF16), ("v", SB_WIDTH, BF16),
           ("z", SSM_WIDTH, F32), ("xbc", CONV_CH, F32))
IN_TAIL = (("cq", Q_LORA, F32), ("ckv", KV_LORA, F32), ("ma", LANES, F32), ("mb", LANES, F32))
IN_SEGS = IN_MAIN + IN_TAIL
IN_MAIN_WIDTH = sum(s[1] for s in IN_MAIN)
IN_TAIL_WIDTH = sum(s[1] for s in IN_TAIL)


def _cparams(*sem):
    return pltpu.CompilerParams(dimension_semantics=sem, vmem_limit_bytes=VMEM_LIMIT)


def _const_spec(shape):
    nd = len(shape)
    return pl.BlockSpec(shape, lambda *_: (0,) * nd, pipeline_mode=pl.Buffered(1))


def _layer_spec(shape, layer):
    nd = len(shape)
    return pl.BlockSpec((None,) + tuple(shape), lambda *_: (layer,) + (0,) * nd, pipeline_mode=pl.Buffered(1))


def _rms(x, g):
    return x * lax.rsqrt(jnp.mean(x * x, axis=-1, keepdims=True) + EPS) * g


def _silu(x):
    return x * (1.0 / (1.0 + jnp.exp(-x)))


def _softplus(x):
    return jnp.maximum(x, 0.0) + jnp.log(1.0 + jnp.exp(-jnp.abs(x)))


def _split3(x):
    hi = x.astype(BF16)
    r1 = x - hi.astype(F32)
    mid = r1.astype(BF16)
    lo = (r1 - mid.astype(F32)).astype(BF16)
    return hi, mid, lo


def _dot(a, b):
    return jnp.dot(a, b, preferred_element_type=F32)


def _dot_nt(a, b):
    return lax.dot_general(a, b, (((1,), (1,)), ((), ())), preferred_element_type=F32)


def _dot_f32_lhs(x, rhs01):
    return sum(_dot(p, rhs01) for p in _split3(x))


def _dot_f32_rhs(lhs01, x):
    return sum(_dot(lhs01, p) for p in _split3(x))


def _inproj_body(h_ref, g_ref, wm_ref, wt_ref, *out_refs):
    hn = _rms(h_ref[...], g_ref[...]).astype(BF16)
    for w_ref, refs in ((wm_ref, out_refs[:len(IN_MAIN)]), (wt_ref, out_refs[len(IN_MAIN):])):
        off = 0
        for ref in refs:
            width = ref.shape[-1]
            ref[...] = _dot(hn, w_ref[:, off:off + width]).astype(ref.dtype)
            off += width


def _inproj(h, g, w_main, w_tail, layer, tm):
    rows = h.shape[0]
    return pl.pallas_call(
        _inproj_body,
        grid=(rows // tm,),
        in_specs=[pl.BlockSpec((tm, D_MODEL), lambda i: (i, 0)),
                  _const_spec((1, D_MODEL)),
                  _layer_spec((D_MODEL, IN_MAIN_WIDTH), layer), _layer_spec((D_MODEL, IN_TAIL_WIDTH), layer)],
        out_specs=[pl.BlockSpec((tm, wd), lambda i: (i, 0)) for _, wd, _ in IN_SEGS],
        out_shape=[jax.ShapeDtypeStruct((rows, wd), dt) for _, wd, dt in IN_SEGS],
        compiler_params=_cparams("parallel"),
        name="inproj",
    )(h, g, w_main, w_tail)


def _sb_body(q_ref, k_ref, v_ref, o_ref):
    T = BLOCK
    NPAIR = SB_WIDTH // LANES
    NH = SB_HEADS
    i = pl.program_id(1)
    first = lax.broadcasted_iota(jnp.int32, (T, LANES), 1) < SB_HEAD_DIM
    q_rows = []
    for p in range(NPAIR):
        qp = q_ref[:, p * LANES:(p + 1) * LANES]
        qz = jnp.zeros_like(qp)
        q_rows.append(jnp.concatenate([jnp.where(first, qp, qz), jnp.where(first, qz, qp)], axis=0))
    ur = lax.broadcasted_iota(jnp.int32, (T, 2 * T), 0)
    uc = lax.broadcasted_iota(jnp.int32, (T, 2 * T), 1)
    u = jnp.where(ur >= uc, -1.0, jnp.where(uc >= T, -1.0, 0.0)).astype(BF16)
    u2 = jnp.concatenate([u, u], axis=0)

    def tile(j, c, accs, masked):
        start = pl.multiple_of(j * T, T)
        z = jnp.concatenate([_dot_nt(q_rows[p], k_ref[pl.ds(start, T), p * LANES:(p + 1) * LANES])
                             for p in range(NPAIR)], axis=0)
        nl = jnp.maximum(z, 0.0) + jnp.log2(1.0 + jnp.exp2(-jnp.abs(z)))
        if masked:
            qpos = i * T + (lax.broadcasted_iota(jnp.int32, (NH * T, T), 0) & (T - 1))
            kpos = start + lax.broadcasted_iota(jnp.int32, (NH * T, T), 1)
            keep = lambda x: jnp.where(kpos < qpos, jnp.where(kpos >= N_PAD, x, 0.0), 0.0)
            nl = keep(nl)
        nl_hi = nl.astype(BF16)
        nl_lo = (nl - nl_hi.astype(F32)).astype(BF16)
        cs = _dot(jnp.concatenate([nl_hi, nl_lo], axis=1), u2)
        w = jnp.exp2(z + c + cs[:, :T])
        if masked:
            w = keep(w)
        w = w.astype(BF16)
        new_accs = []
        for p in range(NPAIR):
            vp = v_ref[pl.ds(start, T), p * LANES:(p + 1) * LANES]
            vz = jnp.zeros_like(vp)
            v_stack = jnp.concatenate([jnp.where(first, vp, vz), jnp.where(first, vz, vp)], axis=0)
            w_pair = jnp.concatenate([w[2 * p * T:(2 * p + 1) * T], w[(2 * p + 1) * T:(2 * p + 2) * T]], axis=1)
            new_accs.append(accs[p] + _dot(w_pair, v_stack))
        c = c + cs[:, T:]
        go = (jnp.max(c) > SB_EXIT).astype(jnp.int32)
        return go, c, new_accs

    def stepper(masked):
        def body(state):
            go, c, accs = tile(state[0], state[2], list(state[3:]), masked)
            return (state[0] - 1, go, c, *accs)
        return body

    c0 = jnp.zeros((NH * T, T), F32)
    acc0 = [jnp.zeros((T, LANES), F32)] * NPAIR
    def diag_only():
        go, c, accs = tile(i, c0, acc0, True)
        return (i - 1, go, c, *accs)

    def diag_and_two():
        _, c, accs = tile(i, c0, acc0, True)
        _, c, accs = tile(i - 1, c, accs, False)
        go, c, accs = tile(i - 2, c, accs, False)
        return (i - 3, go, c, *accs)

    state = lax.cond(i >= 3, diag_and_two, diag_only)

    def two_tiles(state):
        _, c, accs = tile(state[0], state[2], list(state[3:]), False)
        go, c, accs = tile(state[0] - 1, c, accs, False)
        return (state[0] - 2, go, c, *accs)

    state = lax.while_loop(lambda s: jnp.logical_and(s[0] >= 2, s[1] > 0), two_tiles, state)
    state = lax.while_loop(lambda s: jnp.logical_and(s[0] >= 1, s[1] > 0), stepper(False), state)
    state = lax.while_loop(lambda s: jnp.logical_and(s[0] == 0, s[1] > 0), stepper(True), state)
    for p in range(NPAIR):
        o_ref[:, p * LANES:(p + 1) * LANES] = state[3 + p].astype(o_ref.dtype)


def _sb_attention(q, k, v):
    b, L, _ = q.shape
    T = BLOCK
    return pl.pallas_call(
        _sb_body,
        grid=(b, L // T),
        in_specs=[pl.BlockSpec((None, T, SB_WIDTH), lambda bi, i: (bi, i, 0)),
                  pl.BlockSpec((None, L, SB_WIDTH), lambda bi, i: (bi, 0, 0)),
                  pl.BlockSpec((None, L, SB_WIDTH), lambda bi, i: (bi, 0, 0))],
        out_specs=pl.BlockSpec((None, T, SB_WIDTH), lambda bi, i: (bi, i, 0)),
        out_shape=jax.ShapeDtypeStruct((b, L, SB_WIDTH), BF16),
        compiler_params=_cparams("parallel", "parallel"),
        name="sb_attention",
    )(q, k, v)


def _ssd_body(xbc_ref, z_ref, mb_ref, cw_ref, cb_ref, dtb_ref, af_ref, df_ref, ng_ref, o_ref, hist_ref, state_ref):
    c = pl.program_id(1)
    T = BLOCK
    GW = SSM_WIDTH // SSM_GROUPS

    @pl.when(c == 0)
    def _():
        hist_ref[0:HIST, :] = jnp.zeros((HIST, CONV_CH), F32)
        state_ref[...] = jnp.zeros_like(state_ref)

    row = c * T + lax.broadcasted_iota(jnp.int32, (T, 1), 0)
    valid_r = row >= N_PAD
    hist_ref[HIST:HIST + T, :] = jnp.where(valid_r, xbc_ref[...], 0.0)
    conv = cb_ref[...]
    for kk in range(CONV_K):
        off = HIST - (CONV_K - 1) + kk
        conv = conv + cw_ref[kk:kk + 1, :] * hist_ref[off:off + T, :]
    tail = hist_ref[T:T + HIST, :]
    hist_ref[0:HIST, :] = tail
    xc = _silu(conv)
    xs = xc[:, :SSM_WIDTH]
    bm = xc[:, SSM_WIDTH:SSM_WIDTH + SSM_GROUPS * SSM_STATE]
    cm = xc[:, SSM_WIDTH + SSM_GROUPS * SSM_STATE:]

    er = lax.broadcasted_iota(jnp.int32, (LANES, SSM_WIDTH), 0)
    ec = lax.broadcasted_iota(jnp.int32, (LANES, SSM_WIDTH), 1)
    expand = jnp.where(er - DT_LANE0 == ec // SSM_HEAD_DIM, 1.0, 0.0).astype(BF16)
    dt_c = jnp.where(valid_r, _softplus(mb_ref[...] + dtb_ref[...]), 0.0)
    tr = lax.broadcasted_iota(jnp.int32, (T, T), 0)
    tc = lax.broadcasted_iota(jnp.int32, (T, T), 1)
    lower = tr >= tc
    tri = jnp.where(lower, 1.0, 0.0).astype(BF16)
    acs = _dot_f32_rhs(tri, dt_c * af_ref[...])
    acs_last = acs[T - 1:T, :]
    x_dt = xs * _dot_f32_lhs(dt_c, expand)
    x_dt_bf = x_dt.astype(BF16)
    xd_bf = (x_dt * _dot_f32_lhs(jnp.exp(acs_last - acs), expand)).astype(BF16)
    exp_acs = _dot_f32_lhs(jnp.exp(acs), expand)
    chunk_decay = exp_acs[T - 1:T, :]

    acs_t = acs.T

    lane = lax.broadcasted_iota(jnp.int32, (T, LANES), 1)
    first = lane < SSM_HEAD_DIM
    y_parts = []
    for g in range(SSM_GROUPS):
        gs = slice(g * GW, (g + 1) * GW)
        bg = bm[:, g * SSM_STATE:(g + 1) * SSM_STATE]
        cg = cm[:, g * SSM_STATE:(g + 1) * SSM_STATE].astype(BF16)
        cb = _dot_nt(cg, bg.astype(BF16))
        prev = state_ref[:, gs]
        y_off = _dot(cg, prev.astype(BF16)) * exp_acs[:, gs]
        state_ref[:, gs] = prev * chunk_decay[:, gs] + _dot(bg.T.astype(BF16), xd_bf[:, gs])
        for pr in range(GW // LANES):
            h0 = (g * GW + pr * LANES) // SSM_HEAD_DIM
            ps = slice(g * GW + pr * LANES, g * GW + (pr + 1) * LANES)
            xp = x_dt_bf[:, ps]
            yd = []
            for h in (h0, h0 + 1):
                seg = acs[:, DT_LANE0 + h:DT_LANE0 + h + 1] - acs_t[DT_LANE0 + h:DT_LANE0 + h + 1, :]
                m = cb * jnp.exp(jnp.where(lower, seg, -jnp.inf))
                yd.append(_dot(m.astype(BF16), xp))
            y_diag = jnp.where(first, yd[0], yd[1])
            y_parts.append(y_diag + y_off[:, pr * LANES:(pr + 1) * LANES] + xs[:, ps] * df_ref[:, ps])
    y = jnp.concatenate(y_parts, axis=-1)
    o_ref[...] = _rms(y * _silu(z_ref[...]), ng_ref[...]).astype(o_ref.dtype)


def _ssd(xbc, z, mb, cw, cb, dtb_lanes, a_lanes, d_full, ng):
    b, L, _ = xbc.shape
    T = BLOCK
    return pl.pallas_call(
        _ssd_body,
        grid=(b, L // T),
        in_specs=[pl.BlockSpec((None, T, CONV_CH), lambda bi, c: (bi, c, 0)),
                  pl.BlockSpec((None, T, SSM_WIDTH), lambda bi, c: (bi, c, 0)),
                  pl.BlockSpec((None, T, LANES), lambda bi, c: (bi, c, 0)),
                  _const_spec((CONV_K, CONV_CH)), _const_spec((1, CONV_CH)),
                  _const_spec((1, LANES)), _const_spec((1, LANES)),
                  _const_spec((1, SSM_WIDTH)), _const_spec((1, SSM_WIDTH))],
        out_specs=pl.BlockSpec((None, T, SSM_WIDTH), lambda bi, c: (bi, c, 0)),
        out_shape=jax.ShapeDtypeStruct((b, L, SSM_WIDTH), BF16),
        scratch_shapes=[pltpu.VMEM((HIST + T, CONV_CH), F32),
                        pltpu.VMEM((SSM_STATE, SSM_WIDTH), F32)],
        compiler_params=_cparams("parallel", "arbitrary"),
        name="ssd",
    )(xbc, z, mb, cw, cb, dtb_lanes, a_lanes, d_full, ng)


def _mla_prep_body(cq_ref, ckv_ref, ma_ref, mb_ref, gq_ref, gkv_ref, wq_ref, wqr_ref, wk_ref, wv_ref,
                   cosq_ref, sinq_ref, cosk_ref, sink_ref, qx_ref, kx_ref, v_ref):
    scale = math.log2(math.e) / math.sqrt(QK_NOPE + QK_ROPE)
    qn = _rms(cq_ref[...], gq_ref[...]).astype(BF16)
    q = _dot(qn, wq_ref[...])
    qr = _dot(qn, wqr_ref[...])
    cosq = cosq_ref[...]
    sinq = sinq_ref[...]
    kvn = _rms(ckv_ref[...], gkv_ref[...]).astype(BF16)
    kn = _dot(kvn, wk_ref[...])
    v_ref[...] = _dot(kvn, wv_ref[...]).astype(v_ref.dtype)
    k_rope = ma_ref[...] * cosk_ref[...] + mb_ref[...] * sink_ref[...]
    tm = q.shape[0]
    lane = lax.broadcasted_iota(jnp.int32, (tm, LANES), 1)
    pos = pl.program_id(1) * tm + lax.broadcasted_iota(jnp.int32, (tm, LANES), 0)
    mask_lane = lane == MLA_MASK_LANE
    q_bias = jnp.where(mask_lane, 1.0, 0.0)
    k_bias = k_rope + jnp.where(mask_lane, jnp.where(pos < N_PAD, NEG_BIG, 0.0), 0.0)
    for h in range(MLA_HEADS):
        hs = slice(h * LANES, (h + 1) * LANES)
        qx_ref[:, hs] = ((q[:, hs] * cosq + qr[:, hs] * sinq) * scale + q_bias).astype(qx_ref.dtype)
        kx_ref[:, hs] = (kn[:, hs] + k_bias).astype(kx_ref.dtype)


def _mla_prep(cq, ckv, ma, mb, gq, gkv, wq, wqr, wk, wv, layer, tabs, tm):
    b, L, _ = cq.shape
    HW = MLA_HEADS * LANES
    row = lambda wd: pl.BlockSpec((None, tm, wd), lambda bi, i: (bi, i, 0))
    tab = pl.BlockSpec((tm, LANES), lambda bi, i: (i, 0))
    return pl.pallas_call(
        _mla_prep_body,
        grid=(b, L // tm),
        in_specs=[row(Q_LORA), row(KV_LORA), row(LANES), row(LANES),
                  _const_spec((1, Q_LORA)), _const_spec((1, KV_LORA)),
                  _layer_spec((Q_LORA, HW), layer), _layer_spec((Q_LORA, HW), layer),
                  _layer_spec((KV_LORA, HW), layer), _layer_spec((KV_LORA, MLA_WIDTH), layer),
                  tab, tab, tab, tab],
        out_specs=[row(HW), row(HW), row(MLA_WIDTH)],
        out_shape=[jax.ShapeDtypeStruct((b, L, HW), BF16), jax.ShapeDtypeStruct((b, L, HW), BF16),
                   jax.ShapeDtypeStruct((b, L, MLA_WIDTH), BF16)],
        compiler_params=_cparams("parallel", "parallel"),
        name="mla_prep",
    )(cq, ckv, ma, mb, gq, gkv, wq, wqr, wk, wv, *tabs)


def _mla_body(q_ref, k_ref, v_ref, o_ref, s_a, s_b, p_a, p_b, al_a, al_b, m_ref, l_ref, acc_ref, *, tq, n_blocks):
    tk = tq
    i = pl.program_id(2)
    n_lane_tiles = tk // LANES
    s_bufs, p_bufs, al_bufs = (s_a, s_b), (p_a, p_b), (al_a, al_b)
    first_v = lax.broadcasted_iota(jnp.int32, (tk, LANES), 1) < V_HEAD
    rc = min(MLA_ROWS, tq)
    first_c = lax.broadcasted_iota(jnp.int32, (rc, LANES), 1) < V_HEAD
    row_c = lax.broadcasted_iota(jnp.int32, (rc, tk), 0)
    col_c = lax.broadcasted_iota(jnp.int32, (rc, tk), 1)

    m_ref[...] = jnp.full(m_ref.shape, NEG_BIG, F32)
    l_ref[...] = jnp.zeros(l_ref.shape, F32)
    acc_ref[...] = jnp.zeros(acc_ref.shape, F32)

    def scores(t, slot):
        start = pl.multiple_of(t * tk, tk)
        for h in range(2):
            s_bufs[slot][h] = _dot_nt(q_ref[:, h * LANES:(h + 1) * LANES],
                                      k_ref[pl.ds(start, tk), h * LANES:(h + 1) * LANES])

    def softmax(slot, mask):
        m_old_all = [m_ref[0], m_ref[1]]
        l_old_all = [l_ref[0], l_ref[1]]
        m_out, l_out, al_out = ([], []), ([], []), []
        for r0 in range(0, tq, rc):
            rows = slice(r0, r0 + rc)
            alphas = []
            for h in range(2):
                s = s_bufs[slot][h, rows, :]
                if mask == "first":
                    s = jnp.where(col_c <= i * tq + r0 + row_c, s, NEG_BIG)
                elif mask == "diag":
                    s = jnp.where(col_c <= r0 + row_c, s, NEG_BIG)
                m_old = m_old_all[h][rows, :]
                m_new = jnp.maximum(m_old, jnp.max(s, axis=-1, keepdims=True))
                alpha = jnp.exp2(m_old - m_new)
                p = jnp.exp2(s - jnp.concatenate([m_new] * n_lane_tiles, axis=1))
                m_out[h].append(m_new)
                l_out[h].append(alpha * l_old_all[h][rows, :] + jnp.sum(p, axis=-1, keepdims=True))
                p_bufs[slot][rows, h * tk:(h + 1) * tk] = p.astype(BF16)
                alphas.append(alpha)
            al_out.append(jnp.where(first_c, alphas[0], alphas[1]))
        for h in range(2):
            m_ref[h] = jnp.concatenate(m_out[h], axis=0)
            l_ref[h] = jnp.concatenate(l_out[h], axis=0)
        al_bufs[slot][...] = jnp.concatenate(al_out, axis=0)

    def update(t, slot):
        start = pl.multiple_of(t * tk, tk)
        vt = v_ref[pl.ds(start, tk), :]
        vz = jnp.zeros_like(vt)
        v_stack = jnp.concatenate([jnp.where(first_v, vt, vz), jnp.where(first_v, vz, vt)], axis=0)
        acc_ref[...] = al_bufs[slot][...] * acc_ref[...] + _dot(p_bufs[slot][...], v_stack)

    scores(0, 0)
    if n_blocks > 1:
        scores(1, 1)
    softmax(0, "first")

    @pl.when(i == 0)
    def _():
        update(0, 0)

    @pl.when(i > 0)
    def _():
        n_plain = i - 1

        def two_tiles(u, _):
            t = 1 + 2 * u
            scores(t + 1, 0)
            update(t - 1, 0)
            softmax(1, None)
            scores(t + 2, 1)
            update(t, 1)
            softmax(0, None)
            return 0

        lax.fori_loop(0, n_plain // 2, two_tiles, 0)

        @pl.when(n_plain % 2 == 1)
        def _():
            scores(i, 0)
            update(i - 2, 0)
            softmax(1, None)

        for parity in (0, 1):
            @pl.when((i & 1) == parity)
            def _(parity=parity):
                update(i - 1, 1 - parity)
                softmax(parity, "diag")
                update(i, parity)

    first_o = lax.broadcasted_iota(jnp.int32, (tq, LANES), 1) < V_HEAD
    o_ref[...] = (acc_ref[...] / jnp.where(first_o, l_ref[0], l_ref[1])).astype(o_ref.dtype)


def _mla_attention(qx, kx, v, tq):
    b, L, _ = qx.shape
    npairs = MLA_WIDTH // LANES
    return pl.pallas_call(
        functools.partial(_mla_body, tq=tq, n_blocks=L // tq),
        grid=(b, npairs, L // tq),
        in_specs=[pl.BlockSpec((None, tq, 2 * LANES), lambda bi, p, i: (bi, i, p)),
                  pl.BlockSpec((None, L, 2 * LANES), lambda bi, p, i: (bi, 0, p)),
                  pl.BlockSpec((None, L, LANES), lambda bi, p, i: (bi, 0, p))],
        out_specs=pl.BlockSpec((None, tq, LANES), lambda bi, p, i: (bi, i, p)),
        out_shape=jax.ShapeDtypeStruct((b, L, MLA_WIDTH), BF16),
        scratch_shapes=[pltpu.VMEM((2, tq, tq), F32), pltpu.VMEM((2, tq, tq), F32),
                        pltpu.VMEM((tq, 2 * tq), BF16), pltpu.VMEM((tq, 2 * tq), BF16),
                        pltpu.VMEM((tq, LANES), F32), pltpu.VMEM((tq, LANES), F32),
                        pltpu.VMEM((2, tq, LANES), F32), pltpu.VMEM((2, tq, LANES), F32),
                        pltpu.VMEM((tq, LANES), F32)],
        compiler_params=_cparams("parallel", "parallel", "parallel"),
        name="mla_attention",
    )(qx, kx, v)


def _mix_ffn_body(h_ref, osb_ref, ossm_ref, omla_ref, wo_ref, g_ref, wg_ref, wu_ref, wd_ref, o_ref):
    h = h_ref[...]
    h = h + _dot(osb_ref[...], wo_ref[0:SB_WIDTH, :])
    h = h + _dot(ossm_ref[...], wo_ref[SB_WIDTH:SB_WIDTH + SSM_WIDTH, :])
    h = h + _dot(omla_ref[...], wo_ref[SB_WIDTH + SSM_WIDTH:, :])
    fn = _rms(h, g_ref[...]).astype(BF16)
    o_ref[...] = h
    hidden = (_silu(_dot(fn, wg_ref[...])) * _dot(fn, wu_ref[...])).astype(BF16)
    o_ref[...] += _dot(hidden, wd_ref[...])


def _mix_ffn(h, osb, ossm, omla, wo, g, wg, wu, wd, layer, tm):
    rows = h.shape[0]
    row = lambda wd_: pl.BlockSpec((tm, wd_), lambda i: (i, 0))
    return pl.pallas_call(
        _mix_ffn_body,
        grid=(rows // tm,),
        in_specs=[row(D_MODEL), row(SB_WIDTH), row(SSM_WIDTH), row(MLA_WIDTH),
                  _layer_spec((MIX_WIDTH, D_MODEL), layer), _const_spec((1, D_MODEL)),
                  _layer_spec((D_MODEL, D_FF), layer), _layer_spec((D_MODEL, D_FF), layer),
                  _layer_spec((D_FF, D_MODEL), layer)],
        out_specs=row(D_MODEL),
        out_shape=jax.ShapeDtypeStruct((rows, D_MODEL), F32),
        compiler_params=_cparams("parallel"),
        name="mix_ffn",
    )(h, osb, ossm, omla, wo, g, wg, wu, wd)


def _final_norm_body(h_ref, g_ref, o_ref):
    o_ref[...] = _rms(h_ref[...], g_ref[...])


def _final_norm(h, g, b, seq, tm):
    L = h.shape[0] // b
    return pl.pallas_call(
        _final_norm_body,
        grid=(b, seq // tm),
        in_specs=[pl.BlockSpec((pl.Element(tm), pl.Element(D_MODEL)),
                               lambda bi, i: (pl.multiple_of(bi * L + i * tm + (L - seq), BLOCK), 0)),
                  _const_spec((1, D_MODEL))],
        out_specs=pl.BlockSpec((None, tm, D_MODEL), lambda bi, i: (bi, i, 0)),
        out_shape=jax.ShapeDtypeStruct((b, seq, D_MODEL), F32),
        compiler_params=_cparams("parallel", "parallel"),
        name="final_norm",
    )(h, g)


def _rot_half_cols(w):
    half = QK_ROPE // 2
    return jnp.concatenate([-w[..., half:], w[..., :half]], axis=-1)


def _lane_block(cols, lane0):
    n = cols.shape[-1]
    return jnp.pad(cols, ((0, 0),) * (cols.ndim - 1) + ((lane0, LANES - lane0 - n),))


def _prep_in_weights(w_in):
    sb_scale = math.log2(math.e) / math.sqrt(SB_HEAD_DIM)
    col_scale = jnp.concatenate([jnp.full((SB_WIDTH,), sb_scale, F32), jnp.ones((IN_MAIN_WIDTH - SB_WIDTH,), F32)])
    w_main = (w_in[..., :IN_MAIN_WIDTH] * col_scale).astype(BF16)
    o = IN_MAIN_WIDTH
    parts = {}
    for name, wd in (("dt", SSM_HEADS), ("cq", Q_LORA), ("ckv", KV_LORA), ("kr", QK_ROPE)):
        parts[name] = w_in[..., o:o + wd]
        o += wd
    ma = _lane_block(parts["kr"], ROPE_LANE0)
    mb = _lane_block(jnp.concatenate([_rot_half_cols(parts["kr"]), parts["dt"]], axis=-1), ROPE_LANE0)
    w_tail = jnp.concatenate([parts["cq"], parts["ckv"], ma, mb], axis=-1).astype(BF16)
    return w_main, w_tail


def _prep_mla_weights(w_uq, w_ukv):
    depth = w_uq.shape[0]
    dq = QK_NOPE + QK_ROPE
    wq = w_uq.reshape(depth, Q_LORA, MLA_HEADS, dq)
    pad = jnp.zeros((depth, Q_LORA, MLA_HEADS, LANES - dq), F32)
    wq_x = jnp.concatenate([wq, pad], axis=-1)
    wq_r = jnp.concatenate([jnp.zeros((depth, Q_LORA, MLA_HEADS, QK_NOPE), F32),
                            _rot_half_cols(wq[..., QK_NOPE:]), pad], axis=-1)
    wkv = w_ukv.reshape(depth, KV_LORA, MLA_HEADS, QK_NOPE + V_HEAD)
    wk_x = jnp.concatenate([wkv[..., :QK_NOPE], jnp.zeros((depth, KV_LORA, MLA_HEADS, LANES - QK_NOPE), F32)],
                           axis=-1)
    wv = wkv[..., QK_NOPE:]
    flat = lambda a: a.reshape(a.shape[0], a.shape[1], -1).astype(BF16)
    return flat(wq_x), flat(wq_r), flat(wk_x), flat(wv)


def _rope_tables(L):
    pos = (jnp.arange(L) - N_PAD).astype(F32)
    inv = ROPE_BASE ** (-jnp.arange(0, QK_ROPE, 2, dtype=F32) / QK_ROPE)
    ang = pos[:, None] * inv[None, :]
    cos2 = jnp.concatenate([jnp.cos(ang), jnp.cos(ang)], axis=-1)
    sin2 = jnp.concatenate([jnp.sin(ang), jnp.sin(ang)], axis=-1)
    tail = jnp.zeros((L, LANES - QK_NOPE - QK_ROPE), F32)
    head0 = jnp.zeros((L, QK_NOPE), F32)
    cosk = jnp.concatenate([head0, cos2, tail], axis=-1)
    sink = jnp.concatenate([head0, sin2, tail], axis=-1)
    cosq = jnp.concatenate([jnp.ones((L, QK_NOPE), F32), cos2, tail], axis=-1)
    return cosq, sink, cosk, sink


def _per_head_lanes(p):
    return jnp.repeat(p, SSM_HEAD_DIM)[None, :]


def kernel(x, meta_tokens, mix_norm_g, w_in, conv_w, conv_b, dt_bias, a_log, d_skip, ssm_norm_g, q_norm_g, w_uq,
           kv_norm_g, w_ukv, w_out, ffn_norm_g, w_gate, w_up, w_down, final_norm_g):
    b, seq, _ = x.shape
    depth = w_in.shape[0]
    L = seq + BLOCK
    rows = b * L
    tm = _tile(rows, ROW_TILE)
    tm_b = _tile(L, ROW_TILE)
    h = jnp.concatenate([jnp.zeros((b, N_PAD, D_MODEL), F32),
                         jnp.broadcast_to(meta_tokens[None], (b, N_META, D_MODEL)), x], axis=1).reshape(rows, D_MODEL)
    tabs = _rope_tables(L)
    w_in_main, w_in_tail = _prep_in_weights(w_in)
    wq_x, wq_r, wk_x, wv = _prep_mla_weights(w_uq, w_ukv)
    wo_b, wg_b, wu_b, wd_b = (w.astype(BF16) for w in (w_out, w_gate, w_up, w_down))
    r3 = lambda a: a.reshape(b, L, a.shape[-1])
    for i in range(depth):
        q, k, v, z, xbc, cq, ckv, ma, mb = _inproj(h, mix_norm_g[i][None, :], w_in_main, w_in_tail, i, tm)
        o_sb = _sb_attention(r3(q), r3(k), r3(v))
        a = -jnp.exp(a_log[i])
        mb3 = r3(mb)
        o_ssm = _ssd(r3(xbc), r3(z), mb3, conv_w[i], conv_b[i][None, :],
                     _lane_block(dt_bias[i][None, :], DT_LANE0), _lane_block(a[None, :], DT_LANE0),
                     _per_head_lanes(d_skip[i]), ssm_norm_g[i][None, :])
        qx, kx, vm = _mla_prep(r3(cq), r3(ckv), r3(ma), mb3, q_norm_g[i][None, :], kv_norm_g[i][None, :],
                               wq_x, wq_r, wk_x, wv, i, tabs, tm_b)
        o_mla = _mla_attention(qx, kx, vm, tm_b)
        h = _mix_ffn(h, o_sb.reshape(rows, SB_WIDTH), o_ssm.reshape(rows, SSM_WIDTH), o_mla.reshape(rows, MLA_WIDTH),
                     wo_b, ffn_norm_g[i][None, :], wg_b, wu_b, wd_b, i, tm)
    return _final_norm(h, final_norm_g[None, :], b, seq, _tile(seq, NORM_TILE))
```
